```python
import math
import jax, jax.numpy as jnp
from jax import lax
import numpy as np

D_MODEL = 4096
BATCH = 2
SEQ = 8192
DEPTH = 2

N_META = 16
PREFIX = 128
CHUNK = 64
SB_BLOCK = 128
EPS = 1e-6

GLA_HEADS = 4
GLA_DK = 128
GLA_DV = 256
GLA_RANK = 16
GLA_TAU = 16.0
GDN_HEADS = 12
GDN_DK = 128
GDN_DV = 128
CONV_K = 4
SB_HEADS = 12
SB_D = 128

GLA_QK = GLA_HEADS * GLA_DK
GLA_W = GLA_HEADS * GLA_DV
GDN_QK = GDN_HEADS * GDN_DK
GDN_W = GDN_HEADS * GDN_DV
GDN_CONV_CH = 2 * GDN_QK + GDN_W
SB_W = SB_HEADS * SB_D
D_MIX = GLA_W + GDN_W + SB_W
IN_SPLITS = (GLA_QK, GLA_QK, GLA_W, GLA_W, GLA_RANK,
             GDN_QK, GDN_QK, GDN_W, GDN_W, GDN_HEADS, GDN_HEADS,
             SB_W, SB_W, SB_W, SB_W)
D_IN = sum(IN_SPLITS)

kernel_name = 'hymba_gla_gdn_stickbreak_hybrid'


def rmsnorm(x, g):
    xf = x.astype(jnp.float32)
    y = xf * lax.rsqrt(jnp.mean(xf * xf, axis=-1, keepdims=True) + EPS)
    return (y * g.astype(jnp.float32)).astype(x.dtype)


def l2norm(x):
    return x * lax.rsqrt(jnp.sum(x * x, axis=-1, keepdims=True) + EPS)


def to_chunks(x, heads):
    b, l, hd = x.shape
    return x.reshape(b, l // CHUNK, CHUNK, heads, hd // heads).transpose(1, 0, 3, 2, 4)


def from_chunks(x):
    n, b, h, c, d = x.shape
    return x.transpose(1, 0, 3, 2, 4).reshape(b, n * c, h, d)


def causal_conv(x, w):
    return lax.conv_general_dilated(
        x, w[:, None, :].astype(x.dtype), window_strides=(1,),
        padding=[(CONV_K - 1, 0)], dimension_numbers=('NWC', 'WIO', 'NWC'),
        feature_group_count=x.shape[-1])


def gla_group(q, k, v, r, lr, w_gate, b_gate, norm_g):
    f32 = jnp.float32
    bsz, seq, _ = q.shape
    log_a = jax.nn.log_sigmoid((lr @ w_gate + b_gate).astype(f32)) / GLA_TAU
    qc = to_chunks(q.astype(f32) * GLA_DK ** -0.5, GLA_HEADS)
    kc = to_chunks(k.astype(f32), GLA_HEADS)
    vc = to_chunks(v.astype(f32), GLA_HEADS)
    gc = to_chunks(log_a, GLA_HEADS)
    idx = jnp.arange(CHUNK)
    incl = (idx[:, None] >= idx[None, :])[:, :, None]

    def step(state, inp):
        qi, ki, vi, gi = inp
        b = jnp.cumsum(gi, axis=2)
        pair = jnp.exp(jnp.where(incl, b[:, :, :, None, :] - b[:, :, None, :, :], -jnp.inf))
        att = jnp.einsum('bhtd,bhsd,bhtsd->bhts', qi, ki, pair)
        out = (jnp.einsum('bhts,bhsv->bhtv', att, vi)
               + jnp.einsum('bhtd,bhdv->bhtv', qi * jnp.exp(b), state))
        b_last = b[:, :, -1:, :]
        state = (state * jnp.exp(b_last)[:, :, 0, :, None]
                 + jnp.einsum('bhsd,bhsv->bhdv', ki * jnp.exp(b_last - b), vi))
        return state, out

    s0 = jnp.zeros((bsz, GLA_HEADS, GLA_DK, GLA_DV), f32)
    _, o = lax.scan(step, s0, (qc, kc, vc, gc))
    o = rmsnorm(from_chunks(o), norm_g)
    gate = jax.nn.silu(r.astype(f32)).reshape(o.shape)
    return (o * gate).reshape(bsz, seq, GLA_W).astype(q.dtype)


def gdn_group(q, k, v, z, b_raw, a_raw, conv_w, a_log, dt_bias, norm_g):
    f32 = jnp.float32
    bsz, seq, _ = q.shape
    qkv = jax.nn.silu(causal_conv(jnp.concatenate([q, k, v], axis=-1), conv_w)).astype(f32)
    q, k, v = jnp.split(qkv, [GDN_QK, 2 * GDN_QK], axis=-1)
    q = l2norm(q.reshape(bsz, seq, GDN_HEADS, GDN_DK)) * GDN_DK ** -0.5
    k = l2norm(k.reshape(bsz, seq, GDN_HEADS, GDN_DK))
    beta = jax.nn.sigmoid(b_raw.astype(f32))
    g = -jnp.exp(a_log.astype(f32)) * jax.nn.softplus(a_raw.astype(f32) + dt_bias.astype(f32))
    qc = to_chunks(q.reshape(bsz, seq, GDN_QK), GDN_HEADS)
    kc = to_chunks(k.reshape(bsz, seq, GDN_QK), GDN_HEADS)
    vc = to_chunks(v, GDN_HEADS)
    bc = to_chunks(beta, GDN_HEADS)[..., 0]
    G = jnp.cumsum(to_chunks(g, GDN_HEADS)[..., 0], axis=-1)
    idx = jnp.arange(CHUNK)
    incl = idx[:, None] >= idx[None, :]
    strict = idx[:, None] > idx[None, :]
    decay = jnp.exp(jnp.where(incl, G[..., :, None] - G[..., None, :], -jnp.inf))
    kb = kc * bc[..., None]
    m = jnp.where(strict, jnp.einsum('nbhtd,nbhsd->nbhts', kb, kc) * decay, 0.0)
    rhs = jnp.concatenate([vc * bc[..., None], kb * jnp.exp(G)[..., None]], axis=-1)
    sol = lax.linalg.triangular_solve(m + jnp.eye(CHUNK, dtype=f32), rhs,
                                      left_side=True, lower=True)
    u, w = sol[..., :GDN_DV], sol[..., GDN_DV:]
    aqk = jnp.einsum('nbhtd,nbhsd->nbhts', qc, kc) * decay

    def step(state, inp):
        qi, ki, ui, wi, gi, ai = inp
        v_new = ui - jnp.einsum('bhtd,bhdv->bhtv', wi, state)
        out = (jnp.einsum('bhtd,bhdv->bhtv', qi * jnp.exp(gi)[..., None], state)
               + jnp.einsum('bhts,bhsv->bhtv', ai, v_new))
        g_last = gi[..., -1:]
        state = (state * jnp.exp(g_last)[..., None]
                 + jnp.einsum('bhsd,bhsv->bhdv', ki * jnp.exp(g_last - gi)[..., None], v_new))
        return state, out

    s0 = jnp.zeros((bsz, GDN_HEADS, GDN_DK, GDN_DV), f32)
    _, o = lax.scan(step, s0, (qc, kc, u, w, G, aqk))
    o = rmsnorm(from_chunks(o), norm_g)
    gate = jax.nn.silu(z.astype(f32)).reshape(o.shape)
    return (o * gate).reshape(bsz, seq, GDN_W).astype(z.dtype)


def sb_group(q, k, v, gate, norm_g, valid):
    f32 = jnp.float32
    bsz, seq, _ = q.shape
    nb = seq // SB_BLOCK
    scale = SB_D ** -0.5
    kh = k.reshape(bsz, seq, SB_HEADS, SB_D).transpose(0, 2, 1, 3)
    vh = v.reshape(bsz, seq, SB_HEADS, SB_D).transpose(0, 2, 1, 3)
    qb = q.reshape(bsz, nb, SB_BLOCK, SB_HEADS, SB_D).transpose(1, 0, 3, 2, 4)
    key_pos = jnp.arange(seq)

    def block(args):
        i, qblk = args
        t = i * SB_BLOCK + jnp.arange(SB_BLOCK)
        vis = (key_pos[None, :] < t[:, None]) & valid[None, :]
        zz = jnp.einsum('bhqd,bhsd->bhqs', qblk, kh, preferred_element_type=f32) * scale
        log_not = jnp.where(vis, jax.nn.log_sigmoid(-zz), 0.0)
        between = lax.cumsum(log_not, axis=3, reverse=True) - log_not
        a = jnp.exp(jnp.where(vis, jax.nn.log_sigmoid(zz) + between, -jnp.inf))
        return jnp.einsum('bhqs,bhsd->bhqd', a.astype(vh.dtype), vh, preferred_element_type=f32)

    o = lax.map(block, (jnp.arange(nb), qb))
    o = o.transpose(1, 0, 3, 2, 4).reshape(bsz, seq, SB_HEADS, SB_D)
    o = rmsnorm(o, norm_g)
    g = jax.nn.silu(gate.astype(f32)).reshape(o.shape)
    return (o * g).reshape(bsz, seq, SB_W).astype(gate.dtype)


def setup_inputs(seed: int = 0) -> dict:
    key = jax.random.key(seed)
    ks = jax.random.split(key, 16)
    f32 = jnp.float32
    x = jax.random.normal(ks[0], (BATCH, SEQ, D_MODEL), f32)
    meta = jax.random.normal(ks[1], (N_META, D_MODEL), f32)
    norm_g = 1.0 + 0.02 * jax.random.normal(ks[2], (DEPTH, D_MODEL), f32)
    w_in = jax.random.normal(ks[3], (DEPTH, D_MODEL, D_IN), f32) * D_MODEL ** -0.5
    gla_w_gate = jax.random.normal(ks[4], (DEPTH, GLA_RANK, GLA_QK), f32) * GLA_RANK ** -0.5
    gla_b_gate = 0.1 * jax.random.normal(ks[5], (DEPTH, GLA_QK), f32)
    gla_norm_g = 1.0 + 0.02 * jax.random.normal(ks[6], (DEPTH, GLA_DV), f32)
    gdn_conv_w = jax.random.normal(ks[7], (DEPTH, CONV_K, GDN_CONV_CH), f32) * CONV_K ** -0.5
    gdn_a_log = jnp.log(jax.random.uniform(ks[8], (DEPTH, GDN_HEADS), f32, 1.0, 16.0))
    dt = jnp.exp(jax.random.uniform(ks[9], (DEPTH, GDN_HEADS), f32, math.log(1e-3), math.log(1e-1)))
    gdn_dt_bias = dt + jnp.log(-jnp.expm1(-dt))
    gdn_norm_g = 1.0 + 0.02 * jax.random.normal(ks[10], (DEPTH, GDN_DV), f32)
    sb_norm_g = 1.0 + 0.02 * jax.random.normal(ks[11], (DEPTH, SB_D), f32)
    w_out = jax.random.normal(ks[12], (DEPTH, D_MIX, D_MODEL), f32) * D_MIX ** -0.5
    final_g = 1.0 + 0.02 * jax.random.normal(ks[13], (D_MODEL,), f32)
    return {'x': x, 'meta': meta, 'norm_g': norm_g, 'w_in': w_in,
            'gla_w_gate': gla_w_gate, 'gla_b_gate': gla_b_gate, 'gla_norm_g': gla_norm_g,
            'gdn_conv_w': gdn_conv_w, 'gdn_a_log': gdn_a_log, 'gdn_dt_bias': gdn_dt_bias,
            'gdn_norm_g': gdn_norm_g, 'sb_norm_g': sb_norm_g, 'w_out': w_out,
            'final_g': final_g}


def reference(x, meta, norm_g, w_in, gla_w_gate, gla_b_gate, gla_norm_g, gdn_conv_w,
              gdn_a_log, gdn_dt_bias, gdn_norm_g, sb_norm_g, w_out, final_g):
    bsz = x.shape[0]
    pad = jnp.zeros((bsz, PREFIX - N_META, D_MODEL), x.dtype)
    metas = jnp.broadcast_to(meta.astype(x.dtype)[None], (bsz, N_META, D_MODEL))
    h = jnp.concatenate([pad, metas, x], axis=1)
    seq = h.shape[1]
    valid = jnp.arange(seq) >= PREFIX - N_META
    offsets = np.cumsum(IN_SPLITS)[:-1].tolist()
    for l in range(DEPTH):
        u = rmsnorm(h, norm_g[l]) @ w_in[l]
        (gq, gk, gv, gr, glr, dq, dk, dv, dz, db, da,
         sq, sk, sv, sg) = jnp.split(u, offsets, axis=-1)
        o_gla = gla_group(gq, gk, gv, gr, glr, gla_w_gate[l], gla_b_gate[l], gla_norm_g[l])
        o_gdn = gdn_group(dq, dk, dv, dz, db, da, gdn_conv_w[l], gdn_a_log[l],
                          gdn_dt_bias[l], gdn_norm_g[l])
        o_sb = sb_group(sq, sk, sv, sg, sb_norm_g[l], valid)
        h = h + jnp.concatenate([o_gla, o_gdn, o_sb], axis=-1) @ w_out[l]
    return rmsnorm(h, final_g)[:, PREFIX:, :]
```

```python
import functools

import jax
import jax.numpy as jnp
from jax import lax
from jax.experimental import pallas as pl
from jax.experimental.pallas import tpu as pltpu

F32 = jnp.float32
BF16 = jnp.bfloat16
HIGHEST = lax.Precision.HIGHEST

N_META = 16
PREFIX = 128
CHUNK = 64
SB_BLOCK = 128
EPS = 1e-6

GLA_HEADS, GLA_DK, GLA_DV, GLA_RANK, GLA_TAU = 4, 128, 256, 16, 16.0
GDN_HEADS, GDN_DK, GDN_DV, CONV_K = 12, 128, 128, 4
SB_HEADS, SB_D = 12, 128

GLA_QK = GLA_HEADS * GLA_DK
GLA_W = GLA_HEADS * GLA_DV
GDN_QK = GDN_HEADS * GDN_DK
GDN_W = GDN_HEADS * GDN_DV
SB_W = SB_HEADS * SB_D

_IN_SPLITS = (GLA_QK, GLA_QK, GLA_W, GLA_W, GLA_RANK,
              GDN_QK, GDN_QK, GDN_W, GDN_W, GDN_HEADS, GDN_HEADS,
              SB_W, SB_W, SB_W, SB_W)
_IN_OFF = [0]
for _w in _IN_SPLITS:
    _IN_OFF.append(_IN_OFF[-1] + _w)

OFF_GQ, OFF_GK, OFF_GV, OFF_GR = 0, 512, 1024, 2048
OFF_DQ, OFF_DK, OFF_DV, OFF_DZ = 3072, 4608, 6144, 7680
OFF_SQ, OFF_SK, OFF_SV, OFF_SG = 9216, 10752, 12288, 13824
N_MAIN = 15360
LANE = 128
SUBLANE = 8
S_LR, S_B, S_A = 0, GLA_RANK, GLA_RANK + GDN_HEADS

V7X_VMEM_LIMIT_BYTES = 56 * 1024 * 1024


def _pick(n, candidates):
    for c in candidates:
        if n % c == 0:
            return c
    raise ValueError(f"no block size in {candidates} divides {n}")


def _params(*sem):
    return pltpu.CompilerParams(dimension_semantics=sem, vmem_limit_bytes=V7X_VMEM_LIMIT_BYTES)


def _dot(a, b):
    return jnp.dot(a, b, preferred_element_type=F32)


def _dot_nt(a, b):
    return lax.dot_general(a, b, (((1,), (1,)), ((), ())), preferred_element_type=F32)


def _dot_tn(a, b):
    return lax.dot_general(a, b, (((0,), (0,)), ((), ())), preferred_element_type=F32)


def _dot_f32(a, b):
    return jnp.dot(a, b, precision=HIGHEST, preferred_element_type=F32)


def _softplus(x):
    return jnp.maximum(x, 0.0) + jnp.log(1.0 + jnp.exp(-jnp.abs(x)))


def _log_sigmoid(x):
    return jnp.minimum(x, 0.0) - jnp.log(1.0 + jnp.exp(-jnp.abs(x)))


def _sigmoid(x):
    return 1.0 / (1.0 + jnp.exp(-x))


def _head_norm_gate(o, norm_g, gate):
    ms = jnp.mean(o * o, axis=-1, keepdims=True)
    return o * lax.rsqrt(ms + EPS) * norm_g * (gate * _sigmoid(gate))


def _rmsnorm_kernel(x_ref, g_ref, o_ref):
    x = x_ref[...]
    ms = jnp.mean(x * x, axis=-1, keepdims=True)
    o_ref[...] = (x * lax.rsqrt(ms + EPS) * g_ref[...]).astype(o_ref.dtype)


def _rmsnorm(x, g, out_dtype):
    m, d = x.shape
    tm = _pick(m, (512, 256, 128, 64, 8))
    return pl.pallas_call(
        _rmsnorm_kernel,
        grid=(m // tm,),
        in_specs=[pl.BlockSpec((tm, d), lambda i: (i, 0)),
                  pl.BlockSpec((1, d), lambda i: (0, 0))],
        out_specs=pl.BlockSpec((tm, d), lambda i: (i, 0)),
        out_shape=jax.ShapeDtypeStruct((m, d), out_dtype),
        compiler_params=_params("parallel"),
        name="rmsnorm",
    )(x, g.reshape(1, d))


def _matmul_kernel(x_ref, w_ref, o_ref):
    o_ref[...] = _dot(x_ref[...], w_ref[...]).astype(o_ref.dtype)


def _matmul(x, w, out_dtype, name):
    m, k = x.shape
    n = w.shape[1]
    tm = _pick(m, (1664, 1280, 832, 640, 512, 256, 128))
    tn = _pick(n, (512, 256, 128))
    return pl.pallas_call(
        _matmul_kernel,
        grid=(m // tm, n // tn),
        in_specs=[pl.BlockSpec((tm, k), lambda i, j: (i, 0)),
                  pl.BlockSpec((k, tn), lambda i, j: (0, j))],
        out_specs=pl.BlockSpec((tm, tn), lambda i, j: (i, j)),
        out_shape=jax.ShapeDtypeStruct((m, n), out_dtype),
        compiler_params=_params("parallel", "parallel"),
        name=name,
    )(x, w)


def _out_proj_kernel(a1_ref, a2_ref, a3_ref, w1_ref, w2_ref, w3_ref, h_ref, o_ref):
    y = _dot(a1_ref[...], w1_ref[...])
    y = y + _dot(a2_ref[...], w2_ref[...])
    y = y + _dot(a3_ref[...], w3_ref[...])
    o_ref[...] = h_ref[...] + y


def _out_proj(a1, a2, a3, w1, w2, w3, h):
    m, d = h.shape
    tm = _pick(m, (832, 640, 512, 256, 128))
    tn = _pick(d, (1024, 512, 256, 128))
    act = lambda a: pl.BlockSpec((tm, a.shape[1]), lambda i, j: (i, 0))
    wgt = lambda w: pl.BlockSpec((w.shape[0], tn), lambda i, j: (0, j))
    return pl.pallas_call(
        _out_proj_kernel,
        grid=(m // tm, d // tn),
        in_specs=[act(a1), act(a2), act(a3), wgt(w1), wgt(w2), wgt(w3),
                  pl.BlockSpec((tm, tn), lambda i, j: (i, j))],
        out_specs=pl.BlockSpec((tm, tn), lambda i, j: (i, j)),
        out_shape=jax.ShapeDtypeStruct((m, d), F32),
        compiler_params=_params("parallel", "parallel"),
        name="out_proj",
    )(a1, a2, a3, w1, w2, w3, h)


def _gla_kernel(q_ref, k_ref, v_ref, r_ref, s_ref, wg_ref, bg_ref, ng_ref, o_ref, st_ref,
                *, n_chunks):
    @pl.when(pl.program_id(1) == 0)
    def _():
        st_ref[...] = jnp.zeros_like(st_ref)

    row = lax.broadcasted_iota(jnp.int32, (CHUNK, CHUNK), 0)
    col = lax.broadcasted_iota(jnp.int32, (CHUNK, CHUNK), 1)
    causal = row >= col
    tri = jnp.where(causal, 1.0, 0.0).astype(F32)
    scale = GLA_DK ** -0.5

    def chunk(c, carry):
        rows = pl.ds(pl.multiple_of(c * CHUNK, CHUNK), CHUNK)
        x = _dot_f32(s_ref[rows, :], wg_ref[...]) + bg_ref[...]
        g = _log_sigmoid(x) * (1.0 / GLA_TAU)
        b = _dot_f32(tri, g)
        b_last = b[CHUNK - 1:CHUNK, :]
        q = q_ref[rows, :].astype(F32)
        k = k_ref[rows, :].astype(F32)
        qt = (q * scale * jnp.exp(b)).astype(BF16)
        kt = (k * jnp.exp(-b)).astype(BF16)
        kl = (k * jnp.exp(b_last - b)).astype(BF16)
        dec = jnp.exp(b_last)
        for h in range(GLA_HEADS):
            ks = slice(h * GLA_DK, (h + 1) * GLA_DK)
            vs = slice(h * GLA_DV, (h + 1) * GLA_DV)
            vh = v_ref[rows, vs]
            att = jnp.where(causal, _dot_nt(qt[:, ks], kt[:, ks]), 0.0)
            st = st_ref[h]
            o = _dot(att.astype(BF16), vh) + _dot_nt(qt[:, ks], st.astype(BF16))
            st_ref[h] = st * dec[:, ks] + _dot_tn(vh, kl[:, ks])
            gate = r_ref[rows, vs].astype(F32)
            o_ref[rows, vs] = _head_norm_gate(o, ng_ref[...], gate).astype(o_ref.dtype)
        return carry

    lax.fori_loop(0, n_chunks, chunk, 0)


def _gla(u_main, u_small, w_gate_pad, b_gate, norm_g):
    bsz, seq, _ = u_main.shape
    t = _pick(seq, (640, 320, 128, 64))
    col = lambda off, w: pl.BlockSpec((None, t, w), lambda b, i: (b, i, off // w))
    full = lambda a: pl.BlockSpec(a.shape, lambda b, i: (0,) * a.ndim)
    return pl.pallas_call(
        functools.partial(_gla_kernel, n_chunks=t // CHUNK),
        grid=(bsz, seq // t),
        in_specs=[col(OFF_GQ, GLA_QK), col(OFF_GK, GLA_QK), col(OFF_GV, GLA_W), col(OFF_GR, GLA_W),
                  pl.BlockSpec((None, t, LANE), lambda b, i: (b, i, 0)),
                  full(w_gate_pad), full(b_gate), full(norm_g)],
        out_specs=pl.BlockSpec((None, t, GLA_W), lambda b, i: (b, i, 0)),
        out_shape=jax.ShapeDtypeStruct((bsz, seq, GLA_W), BF16),
        scratch_shapes=[pltpu.VMEM((GLA_HEADS, GLA_DV, GLA_DK), F32)],
        compiler_params=_params("parallel", "arbitrary"),
        name="gla",
    )(u_main, u_main, u_main, u_main, u_small, w_gate_pad, b_gate, norm_g)


GDN_PREP_T = 128
HALO = 16


def _gdn_prep_kernel(q_ref, k_ref, v_ref, qh_ref, kh_ref, vh_ref, s_ref, cw_ref, al_ref, dt_ref,
                     eb_ref, eg_ref, qg_ref, kd_ref, u_ref, w_ref, aqk_ref, gl_ref,
                     qs_ref, ks_ref, vs_ref):
    first = pl.program_id(1) == 0
    t = q_ref.shape[0]

    def conv_silu(x_ref, h_ref, w):
        x = x_ref[...].astype(F32)
        halo = jnp.where(first, 0.0, h_ref[...].astype(F32)[HALO - SUBLANE:, :])
        xe = jnp.concatenate([halo, x], axis=0)
        acc = xe * w[CONV_K - 1:CONV_K, :]
        for j in range(1, CONV_K):
            acc = acc + pltpu.roll(xe, j, axis=0) * w[CONV_K - 1 - j:CONV_K - j, :]
        y = acc[SUBLANE:, :]
        return y * _sigmoid(y)

    qc = conv_silu(q_ref, qh_ref, cw_ref[:, 0:GDN_QK])
    kc = conv_silu(k_ref, kh_ref, cw_ref[:, GDN_QK:2 * GDN_QK])
    vs_ref[...] = conv_silu(v_ref, vh_ref, cw_ref[:, 2 * GDN_QK:2 * GDN_QK + GDN_W])
    for h in range(GDN_HEADS):
        sl = slice(h * GDN_DK, (h + 1) * GDN_DK)
        qh = qc[:, sl]
        kh = kc[:, sl]
        qs_ref[:, sl] = qh * (lax.rsqrt(jnp.sum(qh * qh, axis=-1, keepdims=True) + EPS)
                              * GDN_DK ** -0.5)
        ks_ref[:, sl] = kh * lax.rsqrt(jnp.sum(kh * kh, axis=-1, keepdims=True) + EPS)

    s = s_ref[...]
    beta = _sigmoid(s)
    g = -jnp.exp(al_ref[...]) * _softplus(s + dt_ref[...])

    row = lax.broadcasted_iota(jnp.int32, (CHUNK, CHUNK), 0)
    col = lax.broadcasted_iota(jnp.int32, (CHUNK, CHUNK), 1)
    causal = row >= col
    strict = row > col
    tri = jnp.where(causal, 1.0, 0.0).astype(F32)
    zeros_half = jnp.zeros((CHUNK, GDN_DV - CHUNK), F32)

    for c in range(t // CHUNK):
        rows = slice(c * CHUNK, (c + 1) * CHUNK)
        gcum = _dot_f32(tri, g[rows])
        gcum_t = gcum.T
        gb = _dot_f32(gcum, eg_ref[...])
        bb = _dot_f32(beta[rows], eb_ref[...])
        g_last = gb[CHUNK - 1:CHUNK, :]
        eg = jnp.exp(gb)
        q = qs_ref[rows, :]
        k = ks_ref[rows, :]
        v = vs_ref[rows, :]
        kb = k * bb
        qg_ref[rows, :] = (q * eg).astype(BF16)
        kd_ref[rows, :] = (k * jnp.exp(g_last - gb)).astype(BF16)
        gl_ref[c * SUBLANE:(c + 1) * SUBLANE, :] = jnp.broadcast_to(jnp.exp(g_last), (SUBLANE, GDN_W))
        rhs_v = v * bb
        rhs_k = kb * eg
        for h in range(GDN_HEADS):
            sl = slice(h * GDN_DK, (h + 1) * GDN_DK)
            lhs = jnp.concatenate([kb[:, sl], q[:, sl]], axis=0).astype(BF16)
            pr = _dot_nt(lhs, k[:, sl].astype(BF16))
            g_t = gb[:, h * GDN_DK:h * GDN_DK + CHUNK]
            g_s = gcum_t[S_A + h:S_A + h + 1, :]
            dec = jnp.exp(jnp.where(causal, g_t - g_s, -jnp.inf))
            a = jnp.where(strict, -pr[:CHUNK] * dec, 0.0)
            aqk = pr[CHUNK:] * dec
            toff = a
            p = a
            for _ in range(5):
                pb = p.astype(BF16)
                p = _dot(pb, pb)
                toff = toff + p + _dot(toff.astype(BF16), p.astype(BF16))
            tb = toff.astype(BF16)
            rv = rhs_v[:, sl]
            rk = rhs_k[:, sl]
            u_ref[rows, sl] = rv + _dot(tb, rv.astype(BF16))
            w_ref[rows, sl] = (rk + _dot(tb, rk.astype(BF16))).astype(BF16)
            aqk_ref[rows, sl] = jnp.concatenate([aqk, zeros_half], axis=1).astype(BF16)


def _gdn_scan_kernel(qg_ref, kd_ref, u_ref, w_ref, aqk_ref, gl_ref, z_ref, ng_ref, o_ref, st_ref,
                     *, n_chunks, n_heads):
    @pl.when(pl.program_id(2) == 0)
    def _():
        st_ref[...] = jnp.zeros_like(st_ref)

    def chunk(c, carry):
        rows = pl.ds(pl.multiple_of(c * CHUNK, CHUNK), CHUNK)
        grow = pl.ds(pl.multiple_of(c * SUBLANE, SUBLANE), SUBLANE)
        for h in range(n_heads):
            sl = slice(h * GDN_DK, (h + 1) * GDN_DK)
            st = st_ref[h]
            lhs = jnp.concatenate([w_ref[rows, sl], qg_ref[rows, sl]], axis=0)
            ws = _dot(lhs, st.astype(BF16))
            vn = (u_ref[rows, sl] - ws[:CHUNK]).astype(BF16)
            o = ws[CHUNK:] + _dot(aqk_ref[rows, h * GDN_DK:h * GDN_DK + CHUNK], vn)
            st_ref[h] = st * gl_ref[grow, sl][0:1, :] + _dot_tn(kd_ref[rows, sl], vn)
            gate = z_ref[rows, sl].astype(F32)
            o_ref[rows, sl] = _head_norm_gate(o, ng_ref[...], gate).astype(o_ref.dtype)
        return carry

    lax.fori_loop(0, n_chunks, chunk, 0)


def _gdn(u_main, u_small, conv_w, a_log_row, dt_row, e_beta, e_g, norm_g):
    bsz, seq, _ = u_main.shape
    t = GDN_PREP_T
    assert seq % t == 0 and t % HALO == 0
    blk = lambda off: pl.BlockSpec((None, t, GDN_W), lambda b, i: (b, i, off // GDN_W))
    halo = lambda off: pl.BlockSpec(
        (None, HALO, GDN_W), lambda b, i: (b, jnp.maximum(i * (t // HALO) - 1, 0), off // GDN_W))
    full = lambda a: pl.BlockSpec(a.shape, lambda b, i: (0,) * a.ndim)
    out = lambda: pl.BlockSpec((None, t, GDN_W), lambda b, i: (b, i, 0))
    gl_rows = t // CHUNK * SUBLANE
    sds = lambda dt: jax.ShapeDtypeStruct((bsz, seq, GDN_W), dt)
    qg, kd, u, w, aqk, gl = pl.pallas_call(
        _gdn_prep_kernel,
        grid=(bsz, seq // t),
        in_specs=[blk(OFF_DQ), blk(OFF_DK), blk(OFF_DV), halo(OFF_DQ), halo(OFF_DK), halo(OFF_DV),
                  pl.BlockSpec((None, t, LANE), lambda b, i: (b, i, 0)),
                  full(conv_w), full(a_log_row), full(dt_row), full(e_beta), full(e_g)],
        out_specs=[out(), out(), out(), out(), out(),
                   pl.BlockSpec((None, gl_rows, GDN_W), lambda b, i: (b, i, 0))],
        out_shape=[sds(BF16), sds(BF16), sds(F32), sds(BF16), sds(BF16),
                   jax.ShapeDtypeStruct((bsz, seq // CHUNK * SUBLANE, GDN_W), F32)],
        scratch_shapes=[pltpu.VMEM((t, GDN_W), F32)] * 3,
        compiler_params=_params("parallel", "parallel"),
        name="gdn_prep",
    )(u_main, u_main, u_main, u_main, u_main, u_main, u_small, conv_w, a_log_row, dt_row, e_beta, e_g)

    ts = _pick(seq, (640, 320, 128, 64))
    hb = 4
    wid = hb * GDN_DK
    blk = lambda: pl.BlockSpec((None, ts, wid), lambda b, hg, i: (b, i, hg))
    return pl.pallas_call(
        functools.partial(_gdn_scan_kernel, n_chunks=ts // CHUNK, n_heads=hb),
        grid=(bsz, GDN_HEADS // hb, seq // ts),
        in_specs=[blk(), blk(), blk(), blk(), blk(),
                  pl.BlockSpec((None, ts // CHUNK * SUBLANE, wid), lambda b, hg, i: (b, i, hg)),
                  pl.BlockSpec((None, ts, wid), lambda b, hg, i: (b, i, OFF_DZ // wid + hg)),
                  pl.BlockSpec(norm_g.shape, lambda b, hg, i: (0, 0))],
        out_specs=blk(),
        out_shape=jax.ShapeDtypeStruct((bsz, seq, GDN_W), BF16),
        scratch_shapes=[pltpu.VMEM((hb, GDN_DK, GDN_DV), F32)],
        compiler_params=_params("parallel", "parallel", "arbitrary"),
        name="gdn_scan",
    )(qg, kd, u, w, aqk, gl, u_main, norm_g)


def _sb_kernel(q_ref, k_ref, v_ref, g_ref, ng_ref, o_ref, *, n_invalid):
    i = pl.program_id(2)
    tq = tk = SB_BLOCK
    q = q_ref[...]
    row = lax.broadcasted_iota(jnp.int32, (tq, tk), 0)
    col = lax.broadcasted_iota(jnp.int32, (tq, tk), 1)
    jr = lax.broadcasted_iota(jnp.int32, (tk, 2 * tk), 0)
    sc = lax.broadcasted_iota(jnp.int32, (tk, 2 * tk), 1)
    rmat = jnp.where((sc >= tk) | (jr > sc), 1.0, 0.0).astype(BF16)
    scale = SB_D ** -0.5
    qpos = i * tq + row

    def body(jj, carry):
        acc, run = carry
        k0 = pl.multiple_of((i - jj) * tk, tk)
        kj = k_ref[pl.ds(k0, tk), :]
        vj = v_ref[pl.ds(k0, tk), :]
        z = _dot_nt(q, kj) * scale
        kpos = k0 + col
        vis = (kpos < qpos) & (kpos >= n_invalid)
        sp = _softplus(z)
        ln = jnp.where(vis, -sp, 0.0)
        hi = ln.astype(BF16)
        lo = (ln - hi.astype(F32)).astype(BF16)
        cs = _dot(hi, rmat) + _dot(lo, rmat)
        between = cs[:, :tk] + run
        a = jnp.where(vis, jnp.exp(z - sp + between), 0.0)
        acc = acc + _dot(a.astype(BF16), vj)
        return acc, run + cs[:, tk:]

    zeros = jnp.zeros((tq, SB_D), F32)
    acc, _ = lax.fori_loop(0, i + 1, body, (zeros, zeros))
    gate = g_ref[...].astype(F32)
    o_ref[...] = _head_norm_gate(acc, ng_ref[...], gate).astype(o_ref.dtype)


def _sb(u_main, norm_g):
    bsz, seq, _ = u_main.shape
    tq = SB_BLOCK
    qblk = lambda off: pl.BlockSpec((None, tq, SB_D), lambda b, h, i: (b, i, off // SB_D + h))
    kblk = lambda off: pl.BlockSpec((None, seq, SB_D), lambda b, h, i: (b, 0, off // SB_D + h))
    return pl.pallas_call(
        functools.partial(_sb_kernel, n_invalid=PREFIX - N_META),
        grid=(bsz, SB_HEADS, seq // tq),
        in_specs=[qblk(OFF_SQ), kblk(OFF_SK), kblk(OFF_SV), qblk(OFF_SG),
                  pl.BlockSpec(norm_g.shape, lambda b, h, i: (0, 0))],
        out_specs=pl.BlockSpec((None, tq, SB_D), lambda b, h, i: (b, i, h)),
        out_shape=jax.ShapeDtypeStruct((bsz, seq, SB_W), BF16),
        compiler_params=_params("parallel", "parallel", "arbitrary"),
        name="stick_breaking",
    )(u_main, u_main, u_main, u_main, norm_g)


def _place(vec, lane0):
    return jnp.zeros((1, LANE), F32).at[0, lane0:lane0 + vec.shape[0]].set(vec.astype(F32))


def _expand(lane0, heads, width):
    src = jnp.arange(LANE)[:, None]
    dst_head = jnp.arange(heads * width)[None, :] // width
    return (src == lane0 + dst_head).astype(F32)


def _layer(h, norm_g, w_in, gla_w_gate, gla_b_gate, gla_norm_g, gdn_conv_w, gdn_a_log, gdn_dt_bias,
           gdn_norm_g, sb_norm_g, w_out, bsz, seq):
    m = bsz * seq
    o = _IN_OFF
    w_main = jnp.concatenate([w_in[:, :o[4]], w_in[:, o[5]:o[9]], w_in[:, o[11]:]], axis=1).astype(BF16)
    w_small = jnp.concatenate(
        [w_in[:, o[4]:o[5]], w_in[:, o[9]:o[11]],
         jnp.zeros((w_in.shape[0], LANE - GLA_RANK - 2 * GDN_HEADS), w_in.dtype)], axis=1).astype(BF16)

    xn = _rmsnorm(h, norm_g, BF16)
    u_main = _matmul(xn, w_main, BF16, "in_proj").reshape(bsz, seq, N_MAIN)
    u_small = _matmul(xn, w_small, F32, "in_proj_small").reshape(bsz, seq, LANE)

    w_gate_pad = jnp.zeros((LANE, GLA_QK), F32).at[S_LR:S_LR + GLA_RANK].set(gla_w_gate)
    o_gla = _gla(u_main, u_small, w_gate_pad, gla_b_gate.reshape(1, GLA_QK),
                 gla_norm_g.reshape(1, GLA_DV))
    o_gdn = _gdn(u_main, u_small, gdn_conv_w, _place(gdn_a_log, S_A), _place(gdn_dt_bias, S_A),
                 _expand(S_B, GDN_HEADS, GDN_DK), _expand(S_A, GDN_HEADS, GDN_DK),
                 gdn_norm_g.reshape(1, GDN_DV))
    o_sb = _sb(u_main, sb_norm_g.reshape(1, SB_D))

    wo = w_out.astype(BF16)
    return _out_proj(o_gla.reshape(m, GLA_W), o_gdn.reshape(m, GDN_W), o_sb.reshape(m, SB_W),
                     wo[:GLA_W], wo[GLA_W:GLA_W + GDN_W], wo[GLA_W + GDN_W:], h)


def kernel(x, meta, norm_g, w_in, gla_w_gate, gla_b_gate, gla_norm_g, gdn_conv_w, gdn_a_log,
           gdn_dt_bias, gdn_norm_g, sb_norm_g, w_out, final_g):
    bsz, n, d = x.shape
    pad = jnp.zeros((bsz, PREFIX - N_META, d), x.dtype)
    metas = jnp.broadcast_to(meta.astype(x.dtype)[None], (bsz, N_META, d))
    seq = PREFIX + n
    h = jnp.concatenate([pad, metas, x], axis=1).reshape(bsz * seq, d)
    for l in range(norm_g.shape[0]):
        h = _layer(h, norm_g[l], w_in[l], gla_w_gate[l], gla_b_gate[l], gla_norm_g[l], gdn_conv_w[l],
                   gdn_a_log[l], gdn_dt_bias[l], gdn_norm_g[l], sb_norm_g[l], w_out[l], bsz, seq)
    out = _rmsnorm(h, final_g, x.dtype)
    return out.reshape(bsz, seq, d)[:, PREFIX:, :]
```

```python
import functools

import jax
import jax.numpy as jnp
from jax import lax
from jax.experimental import pallas as pl
from jax.experimental.pallas import tpu as pltpu

F32 = jnp.float32
BF16 = jnp.bfloat16
HIGHEST = lax.Precision.HIGHEST

N_META = 16
PREFIX = 256
CHUNK = 64
EPS = 1e-6
LOG2E = 1.4426950408889634

GLA_HEADS, GLA_DK, GLA_DV, GLA_RANK, GLA_TAU = 4, 128, 256, 16, 16.0
GDN_HEADS, GDN_DK, GDN_DV, CONV_K = 12, 128, 128, 4
SB_HEADS, SB_D = 12, 128

GLA_QK = GLA_HEADS * GLA_DK
GLA_W = GLA_HEADS * GLA_DV
GDN_QK = GDN_HEADS * GDN_DK
GDN_W = GDN_HEADS * GDN_DV
SB_W = SB_HEADS * SB_D

_IN_SPLITS = (GLA_QK, GLA_QK, GLA_W, GLA_W, GLA_RANK,
              GDN_QK, GDN_QK, GDN_W, GDN_W, GDN_HEADS, GDN_HEADS,
              SB_W, SB_W, SB_W, SB_W)
_IN_OFF = [0]
for _w in _IN_SPLITS:
    _IN_OFF.append(_IN_OFF[-1] + _w)

OFF_GQ, OFF_GK, OFF_GV, OFF_GR = 0, 512, 1024, 2048
OFF_DQ, OFF_DK, OFF_DV, OFF_DZ = 3072, 4608, 6144, 7680
OFF_SQ, OFF_SK, OFF_SV, OFF_SG = 9216, 10752, 12288, 13824
N_MAIN = 15360
LANE = 128
SUBLANE = 8
S_LR, S_B, S_A = 0, GLA_RANK, GLA_RANK + GDN_HEADS

V7X_VMEM_LIMIT_BYTES = 56 * 1024 * 1024


def _pick(n, candidates):
    for c in candidates:
        if n % c == 0:
            return c
    raise ValueError(f"no block size in {candidates} divides {n}")


def _params(*sem):
    return pltpu.CompilerParams(dimension_semantics=sem, vmem_limit_bytes=V7X_VMEM_LIMIT_BYTES)


def _dot(a, b):
    return jnp.dot(a, b, preferred_element_type=F32)


def _dot_nt(a, b):
    return lax.dot_general(a, b, (((1,), (1,)), ((), ())), preferred_element_type=F32)


def _dot_tn(a, b):
    return lax.dot_general(a, b, (((0,), (0,)), ((), ())), preferred_element_type=F32)


def _dot_f32(a, b):
    return jnp.dot(a, b, precision=HIGHEST, preferred_element_type=F32)


def _softplus(x):
    return jnp.maximum(x, 0.0) + jnp.log(1.0 + jnp.exp(-jnp.abs(x)))


def _log_sigmoid(x):
    return jnp.minimum(x, 0.0) - jnp.log(1.0 + jnp.exp(-jnp.abs(x)))


def _sigmoid(x):
    return 1.0 / (1.0 + jnp.exp(-x))


def _head_norm_gate(o, norm_g, gate):
    ms = jnp.mean(o * o, axis=-1, keepdims=True)
    return o * lax.rsqrt(ms + EPS) * norm_g * (gate * _sigmoid(gate))


def _rmsnorm_kernel(x_ref, g_ref, o_ref):
    x = x_ref[...]
    ms = jnp.mean(x * x, axis=-1, keepdims=True)
    o_ref[...] = (x * lax.rsqrt(ms + EPS) * g_ref[...]).astype(o_ref.dtype)


def _rmsnorm(x, g, out_dtype):
    m, d = x.shape
    tm = _pick(m, (512, 256, 128, 64, 8))
    return pl.pallas_call(
        _rmsnorm_kernel,
        grid=(m // tm,),
        in_specs=[pl.BlockSpec((tm, d), lambda i: (i, 0)),
                  pl.BlockSpec((1, d), lambda i: (0, 0))],
        out_specs=pl.BlockSpec((tm, d), lambda i: (i, 0)),
        out_shape=jax.ShapeDtypeStruct((m, d), out_dtype),
        compiler_params=_params("parallel"),
        name="rmsnorm",
    )(x, g.reshape(1, d))


def _matmul_kernel(x_ref, w_ref, o_ref):
    o_ref[...] = _dot(x_ref[...], w_ref[...]).astype(o_ref.dtype)


def _matmul(x, w, out_dtype, name):
    m, k = x.shape
    n = w.shape[1]
    tm = _pick(m, (1536, 1408, 1024, 768, 512, 256, 128))
    tn = _pick(n, (512, 256, 128))
    return pl.pallas_call(
        _matmul_kernel,
        grid=(m // tm, n // tn),
        in_specs=[pl.BlockSpec((tm, k), lambda i, j: (i, 0)),
                  pl.BlockSpec((k, tn), lambda i, j: (0, j))],
        out_specs=pl.BlockSpec((tm, tn), lambda i, j: (i, j)),
        out_shape=jax.ShapeDtypeStruct((m, n), out_dtype),
        compiler_params=_params("parallel", "parallel"),
        name=name,
    )(x, w)


def _out_proj_kernel(a1_ref, a2_ref, a3_ref, w1_ref, w2_ref, w3_ref, h_ref, o_ref):
    y = _dot(a1_ref[...], w1_ref[...])
    y = y + _dot(a2_ref[...], w2_ref[...])
    y = y + _dot(a3_ref[...], w3_ref[...])
    o_ref[...] = h_ref[...] + y


def _out_proj(a1, a2, a3, w1, w2, w3, h):
    m, d = h.shape
    tm = _pick(m, (768, 512, 256, 128))
    tn = _pick(d, (1024, 512, 256, 128))
    act = lambda a: pl.BlockSpec((tm, a.shape[1]), lambda i, j: (i, 0))
    wgt = lambda w: pl.BlockSpec((w.shape[0], tn), lambda i, j: (0, j))
    return pl.pallas_call(
        _out_proj_kernel,
        grid=(m // tm, d // tn),
        in_specs=[act(a1), act(a2), act(a3), wgt(w1), wgt(w2), wgt(w3),
                  pl.BlockSpec((tm, tn), lambda i, j: (i, j))],
        out_specs=pl.BlockSpec((tm, tn), lambda i, j: (i, j)),
        out_shape=jax.ShapeDtypeStruct((m, d), F32),
        compiler_params=_params("parallel", "parallel"),
        name="out_proj",
    )(a1, a2, a3, w1, w2, w3, h)


def _gla_kernel(q_ref, k_ref, v_ref, r_ref, s_ref, wg_ref, bg_ref, ng_ref, o_ref, st_ref,
                *, n_chunks):
    @pl.when(pl.program_id(1) == 0)
    def _():
        st_ref[...] = jnp.zeros_like(st_ref)

    row = lax.broadcasted_iota(jnp.int32, (CHUNK, CHUNK), 0)
    col = lax.broadcasted_iota(jnp.int32, (CHUNK, CHUNK), 1)
    causal = row >= col
    tri = jnp.where(causal, 1.0, 0.0).astype(F32)
    scale = GLA_DK ** -0.5

    def chunk(c, carry):
        rows = pl.ds(pl.multiple_of(c * CHUNK, CHUNK), CHUNK)
        x = _dot_f32(s_ref[rows, :], wg_ref[...]) + bg_ref[...]
        g = _log_sigmoid(x) * (1.0 / GLA_TAU)
        b = _dot_f32(tri, g)
        b_last = b[CHUNK - 1:CHUNK, :]
        q = q_ref[rows, :].astype(F32)
        k = k_ref[rows, :].astype(F32)
        qt = (q * scale * jnp.exp(b)).astype(BF16)
        kt = (k * jnp.exp(-b)).astype(BF16)
        kl = (k * jnp.exp(b_last - b)).astype(BF16)
        dec = jnp.exp(b_last)
        for h in range(GLA_HEADS):
            ks = slice(h * GLA_DK, (h + 1) * GLA_DK)
            vs = slice(h * GLA_DV, (h + 1) * GLA_DV)
            vh = v_ref[rows, vs]
            att = jnp.where(causal, _dot_nt(qt[:, ks], kt[:, ks]), 0.0)
            st = st_ref[h]
            o = _dot(att.astype(BF16), vh) + _dot_nt(qt[:, ks], st.astype(BF16))
            st_ref[h] = st * dec[:, ks] + _dot_tn(vh, kl[:, ks])
            gate = r_ref[rows, vs].astype(F32)
            o_ref[rows, vs] = _head_norm_gate(o, ng_ref[...], gate).astype(o_ref.dtype)
        return carry

    lax.fori_loop(0, n_chunks, chunk, 0)


def _gla(u_main, u_small, w_gate_pad, b_gate, norm_g):
    bsz, seq, _ = u_main.shape
    t = _pick(seq, (768, 384, 256, 128, 64))
    col = lambda off, w: pl.BlockSpec((None, t, w), lambda b, i: (b, i, off // w))
    full = lambda a: pl.BlockSpec(a.shape, lambda b, i: (0,) * a.ndim)
    return pl.pallas_call(
        functools.partial(_gla_kernel, n_chunks=t // CHUNK),
        grid=(bsz, seq // t),
        in_specs=[col(OFF_GQ, GLA_QK), col(OFF_GK, GLA_QK), col(OFF_GV, GLA_W), col(OFF_GR, GLA_W),
                  pl.BlockSpec((None, t, LANE), lambda b, i: (b, i, 0)),
                  full(w_gate_pad), full(b_gate), full(norm_g)],
        out_specs=pl.BlockSpec((None, t, GLA_W), lambda b, i: (b, i, 0)),
        out_shape=jax.ShapeDtypeStruct((bsz, seq, GLA_W), BF16),
        scratch_shapes=[pltpu.VMEM((GLA_HEADS, GLA_DV, GLA_DK), F32)],
        compiler_params=_params("parallel", "arbitrary"),
        name="gla",
    )(u_main, u_main, u_main, u_main, u_small, w_gate_pad, b_gate, norm_g)


GDN_PREP_T = 128
HALO = 16


def _gdn_prep_kernel(q_ref, k_ref, v_ref, qh_ref, kh_ref, vh_ref, s_ref, cw_ref, al_ref, dt_ref,
                     eb_ref, eg_ref, qg_ref, kd_ref, u_ref, w_ref, aqk_ref, gl_ref,
                     qs_ref, ks_ref, vs_ref):
    first = pl.program_id(1) == 0
    t = q_ref.shape[0]

    def conv_silu(x_ref, h_ref, w):
        x = x_ref[...].astype(F32)
        halo = jnp.where(first, 0.0, h_ref[...].astype(F32)[HALO - SUBLANE:, :])
        xe = jnp.concatenate([halo, x], axis=0)
        acc = xe * w[CONV_K - 1:CONV_K, :]
        for j in range(1, CONV_K):
            acc = acc + pltpu.roll(xe, j, axis=0) * w[CONV_K - 1 - j:CONV_K - j, :]
        y = acc[SUBLANE:, :]
        return y * _sigmoid(y)

    qc = conv_silu(q_ref, qh_ref, cw_ref[:, 0:GDN_QK])
    kc = conv_silu(k_ref, kh_ref, cw_ref[:, GDN_QK:2 * GDN_QK])
    vs_ref[...] = conv_silu(v_ref, vh_ref, cw_ref[:, 2 * GDN_QK:2 * GDN_QK + GDN_W])
    for h in range(GDN_HEADS):
        sl = slice(h * GDN_DK, (h + 1) * GDN_DK)
        qh = qc[:, sl]
        kh = kc[:, sl]
        qs_ref[:, sl] = qh * (lax.rsqrt(jnp.sum(qh * qh, axis=-1, keepdims=True) + EPS)
                              * GDN_DK ** -0.5)
        ks_ref[:, sl] = kh * lax.rsqrt(jnp.sum(kh * kh, axis=-1, keepdims=True) + EPS)

    s = s_ref[...]
    beta = _sigmoid(s)
    g = -jnp.exp(al_ref[...]) * _softplus(s + dt_ref[...])

    row = lax.broadcasted_iota(jnp.int32, (CHUNK, CHUNK), 0)
    col = lax.broadcasted_iota(jnp.int32, (CHUNK, CHUNK), 1)
    causal = row >= col
    strict = row > col
    tri = jnp.where(causal, 1.0, 0.0).astype(F32)
    zeros_half = jnp.zeros((CHUNK, GDN_DV - CHUNK), F32)

    pairs = [(c, h) for c in range(t // CHUNK) for h in range(GDN_HEADS)]
    a_neg, rhs = {}, {}
    for c in range(t // CHUNK):
        rows = slice(c * CHUNK, (c + 1) * CHUNK)
        gcum = _dot_f32(tri, g[rows])
        gcum_t = gcum.T
        gb = _dot_f32(gcum, eg_ref[...])
        bb = _dot_f32(beta[rows], eb_ref[...])
        g_last = gb[CHUNK - 1:CHUNK, :]
        eg = jnp.exp(gb)
        q = qs_ref[rows, :]
        k = ks_ref[rows, :]
        kb = k * bb
        qg_ref[rows, :] = (q * eg).astype(BF16)
        kd_ref[rows, :] = (k * jnp.exp(g_last - gb)).astype(BF16)
        gl_ref[c * SUBLANE:(c + 1) * SUBLANE, :] = jnp.broadcast_to(jnp.exp(g_last), (SUBLANE, GDN_W))
        rhs_v = vs_ref[rows, :] * bb
        rhs_k = kb * eg
        for h in range(GDN_HEADS):
            sl = slice(h * GDN_DK, (h + 1) * GDN_DK)
            lhs = jnp.concatenate([kb[:, sl], q[:, sl]], axis=0).astype(BF16)
            pr = _dot_nt(lhs, k[:, sl].astype(BF16))
            g_t = gb[:, h * GDN_DK:h * GDN_DK + CHUNK]
            g_s = gcum_t[S_A + h:S_A + h + 1, :]
            dec = jnp.exp(jnp.where(causal, g_t - g_s, -jnp.inf))
            a_neg[c, h] = jnp.where(strict, -pr[:CHUNK] * dec, 0.0)
            aqk_ref[rows, sl] = jnp.concatenate([pr[CHUNK:] * dec, zeros_half], axis=1).astype(BF16)
            rhs[c, h] = jnp.concatenate([rhs_v[:, sl], rhs_k[:, sl]], axis=1)
    toff = dict(a_neg)
    p = dict(a_neg)
    for _ in range(5):
        for key in pairs:
            pb = p[key].astype(BF16)
            p[key] = _dot(pb, pb)
        for key in pairs:
            toff[key] = toff[key] + p[key] + _dot(toff[key].astype(BF16), p[key].astype(BF16))
    for c, h in pairs:
        rows = slice(c * CHUNK, (c + 1) * CHUNK)
        sl = slice(h * GDN_DK, (h + 1) * GDN_DK)
        sol = rhs[c, h] + _dot(toff[c, h].astype(BF16), rhs[c, h].astype(BF16))
        u_ref[rows, sl] = sol[:, :GDN_DV]
        w_ref[rows, sl] = sol[:, GDN_DV:].astype(BF16)


def _gdn_scan_kernel(qg_ref, kd_ref, u_ref, w_ref, aqk_ref, gl_ref, z_ref, ng_ref, o_ref, st_ref,
                     *, n_chunks, n_heads):
    @pl.when(pl.program_id(2) == 0)
    def _():
        st_ref[...] = jnp.zeros_like(st_ref)

    def chunk(c, carry):
        rows = pl.ds(pl.multiple_of(c * CHUNK, CHUNK), CHUNK)
        grow = pl.ds(pl.multiple_of(c * SUBLANE, SUBLANE), SUBLANE)
        for h in range(n_heads):
            sl = slice(h * GDN_DK, (h + 1) * GDN_DK)
            st = st_ref[h]
            lhs = jnp.concatenate([w_ref[rows, sl], qg_ref[rows, sl]], axis=0)
            ws = _dot(lhs, st.astype(BF16))
            vn = (u_ref[rows, sl] - ws[:CHUNK]).astype(BF16)
            o = ws[CHUNK:] + _dot(aqk_ref[rows, h * GDN_DK:h * GDN_DK + CHUNK], vn)
            st_ref[h] = st * gl_ref[grow, sl][0:1, :] + _dot_tn(kd_ref[rows, sl], vn)
            gate = z_ref[rows, sl].astype(F32)
            o_ref[rows, sl] = _head_norm_gate(o, ng_ref[...], gate).astype(o_ref.dtype)
        return carry

    lax.fori_loop(0, n_chunks, chunk, 0)


def _gdn(u_main, u_small, conv_w, a_log_row, dt_row, e_beta, e_g, norm_g):
    bsz, seq, _ = u_main.shape
    t = GDN_PREP_T
    assert seq % t == 0 and t % HALO == 0
    blk = lambda off: pl.BlockSpec((None, t, GDN_W), lambda b, i: (b, i, off // GDN_W))
    halo = lambda off: pl.BlockSpec(
        (None, HALO, GDN_W), lambda b, i: (b, jnp.maximum(i * (t // HALO) - 1, 0), off // GDN_W))
    full = lambda a: pl.BlockSpec(a.shape, lambda b, i: (0,) * a.ndim)
    out = lambda: pl.BlockSpec((None, t, GDN_W), lambda b, i: (b, i, 0))
    gl_rows = t // CHUNK * SUBLANE
    sds = lambda dt: jax.ShapeDtypeStruct((bsz, seq, GDN_W), dt)
    qg, kd, u, w, aqk, gl = pl.pallas_call(
        _gdn_prep_kernel,
        grid=(bsz, seq // t),
        in_specs=[blk(OFF_DQ), blk(OFF_DK), blk(OFF_DV), halo(OFF_DQ), halo(OFF_DK), halo(OFF_DV),
                  pl.BlockSpec((None, t, LANE), lambda b, i: (b, i, 0)),
                  full(conv_w), full(a_log_row), full(dt_row), full(e_beta), full(e_g)],
        out_specs=[out(), out(), out(), out(), out(),
                   pl.BlockSpec((None, gl_rows, GDN_W), lambda b, i: (b, i, 0))],
        out_shape=[sds(BF16), sds(BF16), sds(F32), sds(BF16), sds(BF16),
                   jax.ShapeDtypeStruct((bsz, seq // CHUNK * SUBLANE, GDN_W), F32)],
        scratch_shapes=[pltpu.VMEM((t, GDN_W), F32)] * 3,
        compiler_params=_params("parallel", "parallel"),
        name="gdn_prep",
    )(u_main, u_main, u_main, u_main, u_main, u_main, u_small, conv_w, a_log_row, dt_row, e_beta, e_g)

    ts = _pick(seq, (768, 384, 256, 128, 64))
    hb = 4
    wid = hb * GDN_DK
    blk = lambda: pl.BlockSpec((None, ts, wid), lambda b, hg, i: (b, i, hg))
    return pl.pallas_call(
        functools.partial(_gdn_scan_kernel, n_chunks=ts // CHUNK, n_heads=hb),
        grid=(bsz, GDN_HEADS // hb, seq // ts),
        in_specs=[blk(), blk(), blk(), blk(), blk(),
                  pl.BlockSpec((None, ts // CHUNK * SUBLANE, wid), lambda b, hg, i: (b, i, hg)),
                  pl.BlockSpec((None, ts, wid), lambda b, hg, i: (b, i, OFF_DZ // wid + hg)),
                  pl.BlockSpec(norm_g.shape, lambda b, hg, i: (0, 0))],
        out_specs=blk(),
        out_shape=jax.ShapeDtypeStruct((bsz, seq, GDN_W), BF16),
        scratch_shapes=[pltpu.VMEM((hb, GDN_DK, GDN_DV), F32)],
        compiler_params=_params("parallel", "parallel", "arbitrary"),
        name="gdn_scan",
    )(qg, kd, u, w, aqk, gl, u_main, norm_g)


SB_TQ = 512
SB_TK = 256
SB_HALF = SB_TK // 2


def _sb_kernel(q_ref, k_ref, v_ref, g_ref, ng_ref, o_ref, hl_ref, z_ref, a_ref, acc_ref, run_ref,
               *, n_invalid, n_qtiles):
    scale = SB_D ** -0.5
    jr = lax.broadcasted_iota(jnp.int32, (SB_TK, SB_TK), 0)
    sc = lax.broadcasted_iota(jnp.int32, (SB_TK, SB_TK), 1)
    rr = jnp.where((sc >= SB_HALF) | ((jr & (SB_HALF - 1)) >= sc), 1.0, 0.0).astype(BF16)

    def stage_a(r0, m, nrows, mode):
        k0 = pl.multiple_of(m * SB_TK, SB_TK)
        z = _dot_nt(q_ref[pl.ds(r0, nrows), :], k_ref[pl.ds(k0, SB_TK), :]) * scale
        sp = jnp.maximum(z, 0.0) + jnp.log(1.0 + jnp.exp2(jnp.abs(z) * -LOG2E))
        if mode != "full":
            kpos = k0 + lax.broadcasted_iota(jnp.int32, (nrows, SB_TK), 1)
            qpos = r0 + lax.broadcasted_iota(jnp.int32, (nrows, SB_TK), 0)
            if mode == "diag":
                vis = kpos < qpos
            elif mode == "prefix":
                vis = kpos >= n_invalid
            else:
                vis = (kpos < qpos) & (kpos >= n_invalid)
            sp = jnp.where(vis, sp, 0.0)
            z = jnp.where(vis, z, -jnp.inf)
        hi = sp.astype(BF16)
        lo = (sp - hi.astype(F32)).astype(BF16)
        for h in range(2):
            cols = slice(h * SB_HALF, (h + 1) * SB_HALF)
            hl_ref[h, 0:nrows, 0:SB_HALF] = hi[:, cols]
            hl_ref[h, 0:nrows, SB_HALF:SB_TK] = lo[:, cols]
        z_ref[0:nrows, :] = z

    def stage_b(nrows):
        run = run_ref[0:nrows, :]
        for h in (1, 0):
            cols = slice(h * SB_HALF, (h + 1) * SB_HALF)
            cs = _dot(hl_ref[h, 0:nrows, :], rr)
            arg = z_ref[0:nrows, cols] - cs[:, :SB_HALF] - run
            a_ref[0:nrows, cols] = jnp.exp(arg).astype(BF16)
            run = run + cs[:, SB_HALF:]
        run_ref[0:nrows, :] = run

    def stage_c(m, nrows):
        v0 = pl.multiple_of(m * SB_TK, SB_TK)
        acc_ref[0:nrows, :] += _dot(a_ref[0:nrows, :], v_ref[pl.ds(v0, SB_TK), :])

    def reset():
        acc_ref[...] = jnp.zeros_like(acc_ref)
        run_ref[...] = jnp.zeros_like(run_ref)

    def finish(r0, nrows):
        gate = g_ref[pl.ds(r0, nrows), :].astype(F32)
        o_ref[pl.ds(r0, nrows), :] = _head_norm_gate(
            acc_ref[0:nrows, :], ng_ref[...], gate).astype(o_ref.dtype)

    reset()
    stage_a(0, 0, SB_TK, "both")
    stage_b(SB_TK)
    stage_c(0, SB_TK)
    finish(0, SB_TK)

    def qtile(qi, carry):
        r0 = pl.multiple_of(SB_TK + qi * SB_TQ, SB_TK)
        m_hi = (SB_TQ // SB_TK) * (qi + 1)
        reset()
        stage_a(r0, m_hi, SB_TQ, "diag")
        stage_b(SB_TQ)
        stage_a(r0, m_hi - 1, SB_TQ, "diag")

        def step(t, c):
            m = m_hi - 2 - t
            stage_c(m + 2, SB_TQ)
            stage_b(SB_TQ)
            stage_a(r0, m, SB_TQ, "full")
            return c

        lax.fori_loop(0, m_hi - 2, step, 0)
        stage_c(2, SB_TQ)
        stage_b(SB_TQ)
        stage_a(r0, 0, SB_TQ, "prefix")
        stage_c(1, SB_TQ)
        stage_b(SB_TQ)
        stage_c(0, SB_TQ)
        finish(r0, SB_TQ)
        return carry

    lax.fori_loop(0, n_qtiles, qtile, 0)


def _sb(u_main, norm_g, n_invalid):
    bsz, seq, _ = u_main.shape
    assert SB_TQ == 2 * SB_TK and (seq - SB_TK) % SB_TQ == 0 and n_invalid < SB_TK
    blk = lambda off: pl.BlockSpec((None, seq, SB_D), lambda b, h: (b, 0, off // SB_D + h))
    return pl.pallas_call(
        functools.partial(_sb_kernel, n_invalid=n_invalid, n_qtiles=(seq - SB_TK) // SB_TQ),
        grid=(bsz, SB_HEADS),
        in_specs=[blk(OFF_SQ), blk(OFF_SK), blk(OFF_SV), blk(OFF_SG),
                  pl.BlockSpec(norm_g.shape, lambda b, h: (0, 0))],
        out_specs=pl.BlockSpec((None, seq, SB_D), lambda b, h: (b, 0, h)),
        out_shape=jax.ShapeDtypeStruct((bsz, seq, SB_W), BF16),
        scratch_shapes=[pltpu.VMEM((2, SB_TQ, SB_TK), BF16), pltpu.VMEM((SB_TQ, SB_TK), F32),
                        pltpu.VMEM((SB_TQ, SB_TK), BF16), pltpu.VMEM((SB_TQ, SB_D), F32),
                        pltpu.VMEM((SB_TQ, SB_D), F32)],
        compiler_params=_params("parallel", "parallel"),
        name="stick_breaking",
    )(u_main, u_main, u_main, u_main, norm_g)


def _place(vec, lane0):
    return jnp.zeros((1, LANE), F32).at[0, lane0:lane0 + vec.shape[0]].set(vec.astype(F32))


def _expand(lane0, heads, width):
    src = jnp.arange(LANE)[:, None]
    dst_head = jnp.arange(heads * width)[None, :] // width
    return (src == lane0 + dst_head).astype(F32)


def _layer(h, norm_g, w_in, gla_w_gate, gla_b_gate, gla_norm_g, gdn_conv_w, gdn_a_log, gdn_dt_bias,
           gdn_norm_g, sb_norm_g, w_out, bsz, seq):
    m = bsz * seq
    o = _IN_OFF
    w_main = jnp.concatenate([w_in[:, :o[4]], w_in[:, o[5]:o[9]], w_in[:, o[11]:]], axis=1).astype(BF16)
    w_small = jnp.concatenate(
        [w_in[:, o[4]:o[5]], w_in[:, o[9]:o[11]],
         jnp.zeros((w_in.shape[0], LANE - GLA_RANK - 2 * GDN_HEADS), w_in.dtype)], axis=1).astype(BF16)

    xn = _rmsnorm(h, norm_g, BF16)
    u_main = _matmul(xn, w_main, BF16, "in_proj").reshape(bsz, seq, N_MAIN)
    u_small = _matmul(xn, w_small, F32, "in_proj_small").reshape(bsz, seq, LANE)

    w_gate_pad = jnp.zeros((LANE, GLA_QK), F32).at[S_LR:S_LR + GLA_RANK].set(gla_w_gate)
    o_gla = _gla(u_main, u_small, w_gate_pad, gla_b_gate.reshape(1, GLA_QK),
                 gla_norm_g.reshape(1, GLA_DV))
    o_gdn = _gdn(u_main, u_small, gdn_conv_w, _place(gdn_a_log, S_A), _place(gdn_dt_bias, S_A),
                 _expand(S_B, GDN_HEADS, GDN_DK), _expand(S_A, GDN_HEADS, GDN_DK),
                 gdn_norm_g.reshape(1, GDN_DV))
    o_sb = _sb(u_main, sb_norm_g.reshape(1, SB_D), PREFIX - N_META)

    wo = w_out.astype(BF16)
    return _out_proj(o_gla.reshape(m, GLA_W), o_gdn.reshape(m, GDN_W), o_sb.reshape(m, SB_W),
                     wo[:GLA_W], wo[GLA_W:GLA_W + GDN_W], wo[GLA_W + GDN_W:], h)


def kernel(x, meta, norm_g, w_in, gla_w_gate, gla_b_gate, gla_norm_g, gdn_conv_w, gdn_a_log,
           gdn_dt_bias, gdn_norm_g, sb_norm_g, w_out, final_g):
    bsz, n, d = x.shape
    pad = jnp.zeros((bsz, PREFIX - N_META, d), x.dtype)
    metas = jnp.broadcast_to(meta.astype(x.dtype)[None], (bsz, N_META, d))
    seq = PREFIX + n
    h = jnp.concatenate([pad, metas, x], axis=1).reshape(bsz * seq, d)
    for l in range(norm_g.shape[0]):
        h = _layer(h, norm_g[l], w_in[l], gla_w_gate[l], gla_b_gate[l], gla_norm_g[l], gdn_conv_w[l],
                   gdn_a_log[l], gdn_dt_bias[l], gdn_norm_g[l], sb_norm_g[l], w_out[l], bsz, seq)
    out = _rmsnorm(h, final_g, x.dtype)
    return out.reshape(bsz, seq, d)[:, PREFIX:, :]
```

```python
import functools

import jax
import jax.numpy as jnp
from jax import lax
from jax.experimental import pallas as pl
from jax.experimental.pallas import tpu as pltpu

F32 = jnp.float32
BF16 = jnp.bfloat16
HIGHEST = lax.Precision.HIGHEST

N_META = 16
PREFIX = 256
CHUNK = 64
EPS = 1e-6
LOG2E = 1.4426950408889634

GLA_HEADS, GLA_DK, GLA_DV, GLA_RANK, GLA_TAU = 4, 128, 256, 16, 16.0
GDN_HEADS, GDN_DK, GDN_DV, CONV_K = 12, 128, 128, 4
SB_HEADS, SB_D = 12, 128

GLA_QK = GLA_HEADS * GLA_DK
GLA_W = GLA_HEADS * GLA_DV
GDN_QK = GDN_HEADS * GDN_DK
GDN_W = GDN_HEADS * GDN_DV
SB_W = SB_HEADS * SB_D

_IN_SPLITS = (GLA_QK, GLA_QK, GLA_W, GLA_W, GLA_RANK,
              GDN_QK, GDN_QK, GDN_W, GDN_W, GDN_HEADS, GDN_HEADS,
              SB_W, SB_W, SB_W, SB_W)
_IN_OFF = [0]
for _w in _IN_SPLITS:
    _IN_OFF.append(_IN_OFF[-1] + _w)

OFF_GQ, OFF_GK, OFF_GV, OFF_GR = 0, 512, 1024, 2048
OFF_DQ, OFF_DK, OFF_DV, OFF_DZ = 3072, 4608, 6144, 7680
OFF_SQ, OFF_SK, OFF_SV, OFF_SG = 9216, 10752, 12288, 13824
N_MAIN = 15360
LANE = 128
SUBLANE = 8
S_LR, S_B, S_A = 0, GLA_RANK, GLA_RANK + GDN_HEADS

V7X_VMEM_LIMIT_BYTES = 56 * 1024 * 1024


def _pick(n, candidates):
    for c in candidates:
        if n % c == 0:
            return c
    raise ValueError(f"no block size in {candidates} divides {n}")


def _params(*sem):
    return pltpu.CompilerParams(dimension_semantics=sem, vmem_limit_bytes=V7X_VMEM_LIMIT_BYTES)


def _dot(a, b):
    return jnp.dot(a, b, preferred_element_type=F32)


def _dot_nt(a, b):
    return lax.dot_general(a, b, (((1,), (1,)), ((), ())), preferred_element_type=F32)


def _dot_tn(a, b):
    return lax.dot_general(a, b, (((0,), (0,)), ((), ())), preferred_element_type=F32)


def _dot_f32(a, b):
    return jnp.dot(a, b, precision=HIGHEST, preferred_element_type=F32)


def _softplus(x):
    return jnp.maximum(x, 0.0) + jnp.log(1.0 + jnp.exp(-jnp.abs(x)))


def _log_sigmoid(x):
    return jnp.minimum(x, 0.0) - jnp.log(1.0 + jnp.exp(-jnp.abs(x)))


def _sigmoid(x):
    return 1.0 / (1.0 + jnp.exp(-x))


def _head_norm_gate(o, norm_g, gate):
    ms = jnp.mean(o * o, axis=-1, keepdims=True)
    return o * lax.rsqrt(ms + EPS) * norm_g * (gate * _sigmoid(gate))


def _rmsnorm_kernel(x_ref, g_ref, o_ref):
    x = x_ref[...]
    ms = jnp.mean(x * x, axis=-1, keepdims=True)
    o_ref[...] = (x * lax.rsqrt(ms + EPS) * g_ref[...]).astype(o_ref.dtype)


def _rmsnorm(x, g, out_dtype):
    m, d = x.shape
    tm = _pick(m, (512, 256, 128, 64, 8))
    return pl.pallas_call(
        _rmsnorm_kernel,
        grid=(m // tm,),
        in_specs=[pl.BlockSpec((tm, d), lambda i: (i, 0)),
                  pl.BlockSpec((1, d), lambda i: (0, 0))],
        out_specs=pl.BlockSpec((tm, d), lambda i: (i, 0)),
        out_shape=jax.ShapeDtypeStruct((m, d), out_dtype),
        compiler_params=_params("parallel"),
        name="rmsnorm",
    )(x, g.reshape(1, d))


def _final_rmsnorm(h, g, n_skip):
    bsz, seq, d = h.shape
    tm = _pick(n_skip, (256, 128, 64, 8))
    assert (seq - n_skip) % tm == 0
    return pl.pallas_call(
        _rmsnorm_kernel,
        grid=(bsz, (seq - n_skip) // tm),
        in_specs=[pl.BlockSpec((None, tm, d), lambda b, i: (b, i + n_skip // tm, 0)),
                  pl.BlockSpec((1, d), lambda b, i: (0, 0))],
        out_specs=pl.BlockSpec((None, tm, d), lambda b, i: (b, i, 0)),
        out_shape=jax.ShapeDtypeStruct((bsz, seq - n_skip, d), h.dtype),
        compiler_params=_params("parallel", "parallel"),
        name="final_rmsnorm",
    )(h, g.reshape(1, d))


def _matmul_kernel(x_ref, w_ref, o_ref):
    o_ref[...] = _dot(x_ref[...], w_ref[...]).astype(o_ref.dtype)


def _matmul(x, w, out_dtype, name):
    m, k = x.shape
    n = w.shape[1]
    tm = _pick(m, (1536, 1408, 1024, 768, 512, 256, 128))
    tn = _pick(n, (512, 256, 128))
    return pl.pallas_call(
        _matmul_kernel,
        grid=(m // tm, n // tn),
        in_specs=[pl.BlockSpec((tm, k), lambda i, j: (i, 0)),
                  pl.BlockSpec((k, tn), lambda i, j: (0, j))],
        out_specs=pl.BlockSpec((tm, tn), lambda i, j: (i, j)),
        out_shape=jax.ShapeDtypeStruct((m, n), out_dtype),
        compiler_params=_params("parallel", "parallel"),
        name=name,
    )(x, w)


def _out_proj_kernel(a1_ref, a2_ref, a3_ref, w1_ref, w2_ref, w3_ref, h_ref, o_ref):
    y = _dot(a1_ref[...], w1_ref[...])
    y = y + _dot(a2_ref[...], w2_ref[...])
    y = y + _dot(a3_ref[...], w3_ref[...])
    o_ref[...] = h_ref[...] + y


def _out_proj(a1, a2, a3, w1, w2, w3, h):
    m, d = h.shape
    tm = _pick(m, (768, 512, 256, 128))
    tn = _pick(d, (1024, 512, 256, 128))
    act = lambda a: pl.BlockSpec((tm, a.shape[1]), lambda i, j: (i, 0))
    wgt = lambda w: pl.BlockSpec((w.shape[0], tn), lambda i, j: (0, j))
    return pl.pallas_call(
        _out_proj_kernel,
        grid=(m // tm, d // tn),
        in_specs=[act(a1), act(a2), act(a3), wgt(w1), wgt(w2), wgt(w3),
                  pl.BlockSpec((tm, tn), lambda i, j: (i, j))],
        out_specs=pl.BlockSpec((tm, tn), lambda i, j: (i, j)),
        out_shape=jax.ShapeDtypeStruct((m, d), F32),
        compiler_params=_params("parallel", "parallel"),
        name="out_proj",
    )(a1, a2, a3, w1, w2, w3, h)


def _gla_kernel(q_ref, k_ref, v_ref, r_ref, s_ref, wg_ref, bg_ref, ng_ref, o_ref, st_ref,
                *, n_chunks):
    @pl.when(pl.program_id(1) == 0)
    def _():
        st_ref[...] = jnp.zeros_like(st_ref)

    row = lax.broadcasted_iota(jnp.int32, (CHUNK, CHUNK), 0)
    col = lax.broadcasted_iota(jnp.int32, (CHUNK, CHUNK), 1)
    causal = row >= col
    tri = jnp.where(causal, 1.0, 0.0).astype(F32)
    scale = GLA_DK ** -0.5

    def chunk(c, carry):
        rows = pl.ds(pl.multiple_of(c * CHUNK, CHUNK), CHUNK)
        x = _dot_f32(s_ref[rows, :], wg_ref[...]) + bg_ref[...]
        g = _log_sigmoid(x) * (1.0 / GLA_TAU)
        b = _dot_f32(tri, g)
        b_last = b[CHUNK - 1:CHUNK, :]
        q = q_ref[rows, :].astype(F32)
        k = k_ref[rows, :].astype(F32)
        qt = (q * scale * jnp.exp(b)).astype(BF16)
        kt = (k * jnp.exp(-b)).astype(BF16)
        kl = (k * jnp.exp(b_last - b)).astype(BF16)
        dec = jnp.exp(b_last)
        heads = range(GLA_HEADS)
        ks = [slice(h * GLA_DK, (h + 1) * GLA_DK) for h in heads]
        vs = [slice(h * GLA_DV, (h + 1) * GLA_DV) for h in heads]
        att = [jnp.where(causal, _dot_nt(qt[:, ks[h]], kt[:, ks[h]]), 0.0).astype(BF16) for h in heads]
        st = [st_ref[h] for h in heads]
        vh = [v_ref[rows, vs[h]] for h in heads]
        o = [_dot(att[h], vh[h]) + _dot_nt(qt[:, ks[h]], st[h].astype(BF16)) for h in heads]
        for h in heads:
            st_ref[h] = st[h] * dec[:, ks[h]] + _dot_tn(vh[h], kl[:, ks[h]])
        for h in heads:
            gate = r_ref[rows, vs[h]].astype(F32)
            o_ref[rows, vs[h]] = _head_norm_gate(o[h], ng_ref[...], gate).astype(o_ref.dtype)
        return carry

    lax.fori_loop(0, n_chunks, chunk, 0)


def _gla(u_main, u_small, w_gate_pad, b_gate, norm_g):
    bsz, seq, _ = u_main.shape
    t = _pick(seq, (768, 384, 256, 128, 64))
    col = lambda off, w: pl.BlockSpec((None, t, w), lambda b, i: (b, i, off // w))
    full = lambda a: pl.BlockSpec(a.shape, lambda b, i: (0,) * a.ndim)
    return pl.pallas_call(
        functools.partial(_gla_kernel, n_chunks=t // CHUNK),
        grid=(bsz, seq // t),
        in_specs=[col(OFF_GQ, GLA_QK), col(OFF_GK, GLA_QK), col(OFF_GV, GLA_W), col(OFF_GR, GLA_W),
                  pl.BlockSpec((None, t, LANE), lambda b, i: (b, i, 0)),
                  full(w_gate_pad), full(b_gate), full(norm_g)],
        out_specs=pl.BlockSpec((None, t, GLA_W), lambda b, i: (b, i, 0)),
        out_shape=jax.ShapeDtypeStruct((bsz, seq, GLA_W), BF16),
        scratch_shapes=[pltpu.VMEM((GLA_HEADS, GLA_DV, GLA_DK), F32)],
        compiler_params=_params("parallel", "arbitrary"),
        name="gla",
    )(u_main, u_main, u_main, u_main, u_small, w_gate_pad, b_gate, norm_g)


GDN_PREP_T = 128
HALO = 16


def _gdn_prep_kernel(q_ref, k_ref, v_ref, qh_ref, kh_ref, vh_ref, s_ref, cw_ref, al_ref, dt_ref,
                     eb_ref, eg_ref, qg_ref, kd_ref, u_ref, w_ref, aqk_ref, gl_ref,
                     qs_ref, ks_ref, vs_ref):
    first = pl.program_id(1) == 0
    t = q_ref.shape[0]

    def conv_silu(x_ref, h_ref, w):
        x = x_ref[...].astype(F32)
        halo = jnp.where(first, 0.0, h_ref[...].astype(F32)[HALO - SUBLANE:, :])
        xe = jnp.concatenate([halo, x], axis=0)
        acc = xe * w[CONV_K - 1:CONV_K, :]
        for j in range(1, CONV_K):
            acc = acc + pltpu.roll(xe, j, axis=0) * w[CONV_K - 1 - j:CONV_K - j, :]
        y = acc[SUBLANE:, :]
        return y * _sigmoid(y)

    qc = conv_silu(q_ref, qh_ref, cw_ref[:, 0:GDN_QK])
    kc = conv_silu(k_ref, kh_ref, cw_ref[:, GDN_QK:2 * GDN_QK])
    vs_ref[...] = conv_silu(v_ref, vh_ref, cw_ref[:, 2 * GDN_QK:2 * GDN_QK + GDN_W])
    for h in range(GDN_HEADS):
        sl = slice(h * GDN_DK, (h + 1) * GDN_DK)
        qh = qc[:, sl]
        kh = kc[:, sl]
        qs_ref[:, sl] = qh * (lax.rsqrt(jnp.sum(qh * qh, axis=-1, keepdims=True) + EPS)
                              * GDN_DK ** -0.5)
        ks_ref[:, sl] = kh * lax.rsqrt(jnp.sum(kh * kh, axis=-1, keepdims=True) + EPS)

    s = s_ref[...]
    beta = _sigmoid(s)
    g = -jnp.exp(al_ref[...]) * _softplus(s + dt_ref[...])

    row = lax.broadcasted_iota(jnp.int32, (CHUNK, CHUNK), 0)
    col = lax.broadcasted_iota(jnp.int32, (CHUNK, CHUNK), 1)
    causal = row >= col
    strict = row > col
    tri = jnp.where(causal, 1.0, 0.0).astype(F32)
    zeros_half = jnp.zeros((CHUNK, GDN_DV - CHUNK), F32)

    pairs = [(c, h) for c in range(t // CHUNK) for h in range(GDN_HEADS)]
    a_neg, rhs = {}, {}
    for c in range(t // CHUNK):
        rows = slice(c * CHUNK, (c + 1) * CHUNK)
        gcum = _dot_f32(tri, g[rows])
        gcum_t = gcum.T
        gb = _dot_f32(gcum, eg_ref[...])
        bb = _dot_f32(beta[rows], eb_ref[...])
        g_last = gb[CHUNK - 1:CHUNK, :]
        eg = jnp.exp(gb)
        q = qs_ref[rows, :]
        k = ks_ref[rows, :]
        kb = k * bb
        qg_ref[rows, :] = (q * eg).astype(BF16)
        kd_ref[rows, :] = (k * jnp.exp(g_last - gb)).astype(BF16)
        gl_ref[c * SUBLANE:(c + 1) * SUBLANE, :] = jnp.broadcast_to(jnp.exp(g_last), (SUBLANE, GDN_W))
        rhs_v = vs_ref[rows, :] * bb
        rhs_k = kb * eg
        for h in range(GDN_HEADS):
            sl = slice(h * GDN_DK, (h + 1) * GDN_DK)
            lhs = jnp.concatenate([kb[:, sl], q[:, sl]], axis=0).astype(BF16)
            pr = _dot_nt(lhs, k[:, sl].astype(BF16))
            g_t = gb[:, h * GDN_DK:h * GDN_DK + CHUNK]
            g_s = gcum_t[S_A + h:S_A + h + 1, :]
            dec = jnp.exp(jnp.where(causal, g_t - g_s, -jnp.inf))
            a_neg[c, h] = jnp.where(strict, -pr[:CHUNK] * dec, 0.0)
            aqk_ref[rows, sl] = jnp.concatenate([pr[CHUNK:] * dec, zeros_half], axis=1).astype(BF16)
            rhs[c, h] = jnp.concatenate([rhs_v[:, sl], rhs_k[:, sl]], axis=1)
    toff = dict(a_neg)
    p = dict(a_neg)
    for _ in range(5):
        for key in pairs:
            pb = p[key].astype(BF16)
            p[key] = _dot(pb, pb)
        for key in pairs:
            toff[key] = toff[key] + p[key] + _dot(toff[key].astype(BF16), p[key].astype(BF16))
    for c, h in pairs:
        rows = slice(c * CHUNK, (c + 1) * CHUNK)
        sl = slice(h * GDN_DK, (h + 1) * GDN_DK)
        sol = rhs[c, h] + _dot(toff[c, h].astype(BF16), rhs[c, h].astype(BF16))
        u_ref[rows, sl] = sol[:, :GDN_DV]
        w_ref[rows, sl] = sol[:, GDN_DV:].astype(BF16)


def _gdn_scan_kernel(qg_ref, kd_ref, u_ref, w_ref, aqk_ref, gl_ref, z_ref, ng_ref, o_ref, st_ref,
                     *, n_chunks, n_heads):
    @pl.when(pl.program_id(2) == 0)
    def _():
        st_ref[...] = jnp.zeros_like(st_ref)

    def chunk(c, carry):
        rows = pl.ds(pl.multiple_of(c * CHUNK, CHUNK), CHUNK)
        grow = pl.ds(pl.multiple_of(c * SUBLANE, SUBLANE), SUBLANE)
        heads = range(n_heads)
        sls = [slice(h * GDN_DK, (h + 1) * GDN_DK) for h in heads]
        st = [st_ref[h] for h in heads]
        ws = [_dot(jnp.concatenate([w_ref[rows, sls[h]], qg_ref[rows, sls[h]]], axis=0),
                   st[h].astype(BF16)) for h in heads]
        vn = [(u_ref[rows, sls[h]] - ws[h][:CHUNK]).astype(BF16) for h in heads]
        o = [ws[h][CHUNK:] + _dot(aqk_ref[rows, h * GDN_DK:h * GDN_DK + CHUNK], vn[h]) for h in heads]
        for h in heads:
            st_ref[h] = st[h] * gl_ref[grow, sls[h]][0:1, :] + _dot_tn(kd_ref[rows, sls[h]], vn[h])
        for h in heads:
            gate = z_ref[rows, sls[h]].astype(F32)
            o_ref[rows, sls[h]] = _head_norm_gate(o[h], ng_ref[...], gate).astype(o_ref.dtype)
        return carry

    lax.fori_loop(0, n_chunks, chunk, 0)


def _gdn(u_main, u_small, conv_w, a_log_row, dt_row, e_beta, e_g, norm_g):
    bsz, seq, _ = u_main.shape
    t = GDN_PREP_T
    assert seq % t == 0 and t % HALO == 0
    blk = lambda off: pl.BlockSpec((None, t, GDN_W), lambda b, i: (b, i, off // GDN_W))
    halo = lambda off: pl.BlockSpec(
        (None, HALO, GDN_W), lambda b, i: (b, jnp.maximum(i * (t // HALO) - 1, 0), off // GDN_W))
    full = lambda a: pl.BlockSpec(a.shape, lambda b, i: (0,) * a.ndim)
    out = lambda: pl.BlockSpec((None, t, GDN_W), lambda b, i: (b, i, 0))
    gl_rows = t // CHUNK * SUBLANE
    sds = lambda dt: jax.ShapeDtypeStruct((bsz, seq, GDN_W), dt)
    qg, kd, u, w, aqk, gl = pl.pallas_call(
        _gdn_prep_kernel,
        grid=(bsz, seq // t),
        in_specs=[blk(OFF_DQ), blk(OFF_DK), blk(OFF_DV), halo(OFF_DQ), halo(OFF_DK), halo(OFF_DV),
                  pl.BlockSpec((None, t, LANE), lambda b, i: (b, i, 0)),
                  full(conv_w), full(a_log_row), full(dt_row), full(e_beta), full(e_g)],
        out_specs=[out(), out(), out(), out(), out(),
                   pl.BlockSpec((None, gl_rows, GDN_W), lambda b, i: (b, i, 0))],
        out_shape=[sds(BF16), sds(BF16), sds(F32), sds(BF16), sds(BF16),
                   jax.ShapeDtypeStruct((bsz, seq // CHUNK * SUBLANE, GDN_W), F32)],
        scratch_shapes=[pltpu.VMEM((t, GDN_W), F32)] * 3,
        compiler_params=_params("parallel", "parallel"),
        name="gdn_prep",
    )(u_main, u_main, u_main, u_main, u_main, u_main, u_small, conv_w, a_log_row, dt_row, e_beta, e_g)

    ts = _pick(seq, (384, 256, 128, 64))
    hb = GDN_HEADS
    wid = hb * GDN_DK
    blk = lambda: pl.BlockSpec((None, ts, wid), lambda b, hg, i: (b, i, hg))
    return pl.pallas_call(
        functools.partial(_gdn_scan_kernel, n_chunks=ts // CHUNK, n_heads=hb),
        grid=(bsz, GDN_HEADS // hb, seq // ts),
        in_specs=[blk(), blk(), blk(), blk(), blk(),
                  pl.BlockSpec((None, ts // CHUNK * SUBLANE, wid), lambda b, hg, i: (b, i, hg)),
                  pl.BlockSpec((None, ts, wid), lambda b, hg, i: (b, i, OFF_DZ // wid + hg)),
                  pl.BlockSpec(norm_g.shape, lambda b, hg, i: (0, 0))],
        out_specs=blk(),
        out_shape=jax.ShapeDtypeStruct((bsz, seq, GDN_W), BF16),
        scratch_shapes=[pltpu.VMEM((hb, GDN_DK, GDN_DV), F32)],
        compiler_params=_params("parallel", "parallel", "arbitrary"),
        name="gdn_scan",
    )(qg, kd, u, w, aqk, gl, u_main, norm_g)


SB_RB = 128
SB_LOOK = 2
SB_QT = 2 * SB_RB
SB_EXP_ZERO = 104.0


def _sb_kernel(q_ref, k_ref, v_ref, g_ref, ng_ref, o_ref, acc_ref, run_ref, *, n_invalid):
    seq = q_ref.shape[0]
    rb = SB_RB
    scale = SB_D ** -0.5
    jr = lax.broadcasted_iota(jnp.int32, (2 * rb, 2 * rb), 0)
    sc = lax.broadcasted_iota(jnp.int32, (2 * rb, 2 * rb), 1)
    rr = jnp.where((sc >= rb) | ((jr & (rb - 1)) >= sc), 1.0, 0.0).astype(BF16)
    row_i = lax.broadcasted_iota(jnp.int32, (rb, rb), 0)
    col_i = lax.broadcasted_iota(jnp.int32, (rb, rb), 1)
    below_diag = col_i < row_i

    def softplus(z):
        return jnp.maximum(z, 0.0) + jnp.log(1.0 + jnp.exp2(jnp.abs(z) * -LOG2E))

    def weights(z, sp, run):
        hi = sp.astype(BF16)
        lo = (sp - hi.astype(F32)).astype(BF16)
        cs = _dot(jnp.concatenate([hi, lo], axis=1), rr)
        return jnp.exp(z - cs[:, :rb] - run).astype(BF16), cs[:, rb:]

    def mask(z, sp, vis):
        return jnp.where(vis, z, -jnp.inf), jnp.where(vis, sp, 0.0)

    def finish(rows, acc):
        gate = g_ref[rows, :].astype(F32)
        o_ref[rows, :] = _head_norm_gate(acc, ng_ref[...], gate).astype(o_ref.dtype)

    def row_block(zs, key0, check_valid):
        run = jnp.zeros((rb, rb), F32)
        parts = [None] * len(zs)
        for c in reversed(range(len(zs))):
            z, sp = zs[c], softplus(zs[c])
            vis = below_diag if c == len(zs) - 1 else None
            if check_valid:
                ok = (key0 + c * rb + col_i) >= n_invalid
                vis = ok if vis is None else (vis & ok)
            if vis is not None:
                z, sp = mask(z, sp, vis)
            parts[c], tot = weights(z, sp, run)
            run = run + tot
        return jnp.concatenate(parts, axis=1), run

    def qtile(r0, looks, check_valid):
        k0 = r0 - looks[0] * rb
        nk = looks[0] + 2
        zz = _dot_nt(q_ref[pl.ds(r0, SB_QT), :], k_ref[pl.ds(k0, nk * rb), :]) * scale
        run_min = None
        for r in range(2):
            c0 = r - looks[r] + looks[0]
            zs = [zz[r * rb:(r + 1) * rb, c * rb:(c + 1) * rb] for c in range(c0, c0 + looks[r] + 1)]
            a, run = row_block(zs, k0 + c0 * rb, check_valid)
            acc = _dot(a, v_ref[pl.ds(k0 + c0 * rb, (looks[r] + 1) * rb), :])
            rows = pl.ds(r0 + r * rb, rb)
            acc_ref[rows, :] = acc
            run_ref[rows, :] = run
            finish(rows, acc)
            run_min = run if run_min is None else jnp.minimum(run_min, run)
        return run_min

    first_open = n_invalid // rb + SB_LOOK + 1
    assert first_open % 2 == 0 and first_open * rb <= seq and (first_open - SB_LOOK) * rb >= n_invalid
    qtile(0, (0, 1), True)
    for qi in range(1, first_open // 2):
        qtile(qi * SB_QT, (SB_LOOK, SB_LOOK), True)

    def first_pass(qi, run_min):
        r0 = pl.multiple_of(qi * SB_QT, SB_QT)
        return jnp.minimum(run_min, qtile(r0, (SB_LOOK, SB_LOOK), False))

    run_min = lax.fori_loop(first_open // 2, seq // SB_QT, first_pass,
                            jnp.full((rb, rb), 2 * SB_EXP_ZERO, F32), unroll=4)

    @pl.when(jnp.min(run_min) < SB_EXP_ZERO)
    def _():
        def second_pass(b, carry):
            rows = pl.ds(pl.multiple_of(b * rb, rb), rb)

            def unfinished(state):
                j, lowest = state
                return (j >= 0) & (lowest < SB_EXP_ZERO)

            def sub_block(state):
                j, _ = state
                keys = pl.ds(pl.multiple_of(j * rb, rb), rb)
                z = _dot_nt(q_ref[rows, :], k_ref[keys, :]) * scale
                z, sp = mask(z, softplus(z), (j * rb + col_i) >= n_invalid)
                run = run_ref[rows, :]
                a, tot = weights(z, sp, run)
                acc_ref[rows, :] += _dot(a, v_ref[keys, :])
                run_ref[rows, :] = run + tot
                return j - 1, jnp.min(run + tot)

            lax.while_loop(unfinished, sub_block, (b - SB_LOOK - 1, jnp.min(run_ref[rows, :])))
            finish(rows, acc_ref[rows, :])
            return carry

        lax.fori_loop(first_open, seq // rb, second_pass, 0)


def _sb(u_main, norm_g, n_invalid):
    bsz, seq, _ = u_main.shape
    assert seq % SB_QT == 0 and SB_D == SB_RB
    blk = lambda off: pl.BlockSpec((None, seq, SB_D), lambda b, h: (b, 0, off // SB_D + h))
    return pl.pallas_call(
        functools.partial(_sb_kernel, n_invalid=n_invalid),
        grid=(bsz, SB_HEADS),
        in_specs=[blk(OFF_SQ), blk(OFF_SK), blk(OFF_SV), blk(OFF_SG),
                  pl.BlockSpec(norm_g.shape, lambda b, h: (0, 0))],
        out_specs=pl.BlockSpec((None, seq, SB_D), lambda b, h: (b, 0, h)),
        out_shape=jax.ShapeDtypeStruct((bsz, seq, SB_W), BF16),
        scratch_shapes=[pltpu.VMEM((seq, SB_D), F32), pltpu.VMEM((seq, SB_D), F32)],
        compiler_params=_params("parallel", "parallel"),
        name="stick_breaking",
    )(u_main, u_main, u_main, u_main, norm_g)


def _place(vec, lane0):
    return jnp.zeros((1, LANE), F32).at[0, lane0:lane0 + vec.shape[0]].set(vec.astype(F32))


def _expand(lane0, heads, width):
    src = jnp.arange(LANE)[:, None]
    dst_head = jnp.arange(heads * width)[None, :] // width
    return (src == lane0 + dst_head).astype(F32)


def _layer(h, norm_g, w_in, gla_w_gate, gla_b_gate, gla_norm_g, gdn_conv_w, gdn_a_log, gdn_dt_bias,
           gdn_norm_g, sb_norm_g, w_out, bsz, seq):
    m = bsz * seq
    o = _IN_OFF
    w_main = jnp.concatenate([w_in[:, :o[4]], w_in[:, o[5]:o[9]], w_in[:, o[11]:]], axis=1).astype(BF16)
    w_small = jnp.concatenate(
        [w_in[:, o[4]:o[5]], w_in[:, o[9]:o[11]],
         jnp.zeros((w_in.shape[0], LANE - GLA_RANK - 2 * GDN_HEADS), w_in.dtype)], axis=1).astype(BF16)

    xn = _rmsnorm(h, norm_g, BF16)
    u_main = _matmul(xn, w_main, BF16, "in_proj").reshape(bsz, seq, N_MAIN)
    u_small = _matmul(xn, w_small, F32, "in_proj_small").reshape(bsz, seq, LANE)

    w_gate_pad = jnp.zeros((LANE, GLA_QK), F32).at[S_LR:S_LR + GLA_RANK].set(gla_w_gate)
    o_gla = _gla(u_main, u_small, w_gate_pad, gla_b_gate.reshape(1, GLA_QK),
                 gla_norm_g.reshape(1, GLA_DV))
    o_gdn = _gdn(u_main, u_small, gdn_conv_w, _place(gdn_a_log, S_A), _place(gdn_dt_bias, S_A),
                 _expand(S_B, GDN_HEADS, GDN_DK), _expand(S_A, GDN_HEADS, GDN_DK),
                 gdn_norm_g.reshape(1, GDN_DV))
    o_sb = _sb(u_main, sb_norm_g.reshape(1, SB_D), PREFIX - N_META)

    wo = w_out.astype(BF16)
    return _out_proj(o_gla.reshape(m, GLA_W), o_gdn.reshape(m, GDN_W), o_sb.reshape(m, SB_W),
                     wo[:GLA_W], wo[GLA_W:GLA_W + GDN_W], wo[GLA_W + GDN_W:], h)


def kernel(x, meta, norm_g, w_in, gla_w_gate, gla_b_gate, gla_norm_g, gdn_conv_w, gdn_a_log,
           gdn_dt_bias, gdn_norm_g, sb_norm_g, w_out, final_g):
    bsz, n, d = x.shape
    pad = jnp.zeros((bsz, PREFIX - N_META, d), x.dtype)
    metas = jnp.broadcast_to(meta.astype(x.dtype)[None], (bsz, N_META, d))
    seq = PREFIX + n
    h = jnp.concatenate([pad, metas, x], axis=1).reshape(bsz * seq, d)
    for l in range(norm_g.shape[0]):
        h = _layer(h, norm_g[l], w_in[l], gla_w_gate[l], gla_b_gate[l], gla_norm_g[l], gdn_conv_w[l],
                   gdn_a_log[l], gdn_dt_bias[l], gdn_norm_g[l], sb_norm_g[l], w_out[l], bsz, seq)
    return _final_rmsnorm(h.reshape(bsz, seq, d), final_g, PREFIX)
```

```python
import functools

import jax
import jax.numpy as jnp
from jax import lax
from jax.experimental import pallas as pl
from jax.experimental.pallas import tpu as pltpu

F32 = jnp.float32
BF16 = jnp.bfloat16
HIGHEST = lax.Precision.HIGHEST

N_META = 16
PREFIX = 256
CHUNK = 64
EPS = 1e-6
LOG2E = 1.4426950408889634

GLA_HEADS, GLA_DK, GLA_DV, GLA_RANK, GLA_TAU = 4, 128, 256, 16, 16.0
GDN_HEADS, GDN_DK, GDN_DV, CONV_K = 12, 128, 128, 4
SB_HEADS, SB_D = 12, 128

GLA_QK = GLA_HEADS * GLA_DK
GLA_W = GLA_HEADS * GLA_DV
GDN_QK = GDN_HEADS * GDN_DK
GDN_W = GDN_HEADS * GDN_DV
SB_W = SB_HEADS * SB_D

_IN_SPLITS = (GLA_QK, GLA_QK, GLA_W, GLA_W, GLA_RANK,
              GDN_QK, GDN_QK, GDN_W, GDN_W, GDN_HEADS, GDN_HEADS,
              SB_W, SB_W, SB_W, SB_W)
_IN_OFF = [0]
for _w in _IN_SPLITS:
    _IN_OFF.append(_IN_OFF[-1] + _w)

OFF_GQ, OFF_GK, OFF_GV, OFF_GR = 0, 512, 1024, 2048
OFF_DQ, OFF_DK, OFF_DV, OFF_DZ = 0, 1536, 3072, 4608
OFF_SQ, OFF_SK, OFF_SV, OFF_SG = 0, 1536, 3072, 4608
LANE = 128
SUBLANE = 8
S_LR, S_B, S_A = 0, GLA_RANK, GLA_RANK + GDN_HEADS

V7X_VMEM_LIMIT_BYTES = 56 * 1024 * 1024


def _pick(n, candidates):
    for c in candidates:
        if n % c == 0:
            return c
    raise ValueError(f"no block size in {candidates} divides {n}")


def _params(*sem):
    return pltpu.CompilerParams(dimension_semantics=sem, vmem_limit_bytes=V7X_VMEM_LIMIT_BYTES)


def _dot(a, b):
    return jnp.dot(a, b, preferred_element_type=F32)


def _dot_nt(a, b):
    return lax.dot_general(a, b, (((1,), (1,)), ((), ())), preferred_element_type=F32)


def _dot_tn(a, b):
    return lax.dot_general(a, b, (((0,), (0,)), ((), ())), preferred_element_type=F32)


def _split_bf16(x, n):
    parts = []
    for _ in range(n):
        p = x.astype(BF16)
        parts.append(p)
        x = x - p.astype(F32)
    return parts


def _dot_f32_by_exact(a, b_exact):
    hi, mid, lo = _split_bf16(a, 3)
    return _dot(hi, b_exact) + _dot(mid, b_exact) + _dot(lo, b_exact)


def _dot_exact_by_f32(a_exact, b):
    hi, mid, lo = _split_bf16(b, 3)
    return _dot(a_exact, hi) + _dot(a_exact, mid) + _dot(a_exact, lo)


def _dot_split2(a_parts, b_parts):
    return _dot(a_parts[0], b_parts[0]) + _dot(a_parts[0], b_parts[1]) + _dot(a_parts[1], b_parts[0])


def _softplus(x):
    return jnp.maximum(x, 0.0) + jnp.log(1.0 + jnp.exp(-jnp.abs(x)))


def _log_sigmoid(x):
    return jnp.minimum(x, 0.0) - jnp.log(1.0 + jnp.exp(-jnp.abs(x)))


def _sigmoid(x):
    return 1.0 / (1.0 + jnp.exp(-x))


def _head_norm_gate(o, norm_g, gate):
    ms = jnp.mean(o * o, axis=-1, keepdims=True)
    return o * lax.rsqrt(ms + EPS) * norm_g * (gate * _sigmoid(gate))


def _rmsnorm_kernel(x_ref, g_ref, o_ref):
    x = x_ref[...]
    ms = jnp.mean(x * x, axis=-1, keepdims=True)
    o_ref[...] = (x * lax.rsqrt(ms + EPS) * g_ref[...]).astype(o_ref.dtype)


def _rmsnorm(x, g, out_dtype):
    m, d = x.shape
    tm = _pick(m, (512, 256, 128, 64, 8))
    return pl.pallas_call(
        _rmsnorm_kernel,
        grid=(m // tm,),
        in_specs=[pl.BlockSpec((tm, d), lambda i: (i, 0)),
                  pl.BlockSpec((1, d), lambda i: (0, 0))],
        out_specs=pl.BlockSpec((tm, d), lambda i: (i, 0)),
        out_shape=jax.ShapeDtypeStruct((m, d), out_dtype),
        compiler_params=_params("parallel"),
        name="rmsnorm",
    )(x, g.reshape(1, d))


def _embed_norm_kernel(x_ref, p_ref, g_ref, h_ref, o_ref):
    x = jnp.where(pl.program_id(1) == 0, p_ref[...], x_ref[...])
    h_ref[...] = x
    ms = jnp.mean(x * x, axis=-1, keepdims=True)
    o_ref[...] = (x * lax.rsqrt(ms + EPS) * g_ref[...]).astype(o_ref.dtype)


def _embed_norm(x, prefix, g):
    bsz, n, d = x.shape
    tm = prefix.shape[0]
    assert n % tm == 0
    blk = lambda: pl.BlockSpec((None, tm, d), lambda b, i: (b, i, 0))
    return pl.pallas_call(
        _embed_norm_kernel,
        grid=(bsz, n // tm + 1),
        in_specs=[pl.BlockSpec((None, tm, d), lambda b, i: (b, jnp.maximum(i - 1, 0), 0)),
                  pl.BlockSpec((tm, d), lambda b, i: (0, 0)),
                  pl.BlockSpec((1, d), lambda b, i: (0, 0))],
        out_specs=[blk(), blk()],
        out_shape=[jax.ShapeDtypeStruct((bsz, n + tm, d), x.dtype),
                   jax.ShapeDtypeStruct((bsz, n + tm, d), BF16)],
        compiler_params=_params("parallel", "parallel"),
        name="embed_norm",
    )(x, prefix, g.reshape(1, d))


def _final_rmsnorm(h, g, n_skip):
    bsz, seq, d = h.shape
    tm = _pick(n_skip, (256, 128, 64, 8))
    assert (seq - n_skip) % tm == 0
    return pl.pallas_call(
        _rmsnorm_kernel,
        grid=(bsz, (seq - n_skip) // tm),
        in_specs=[pl.BlockSpec((None, tm, d), lambda b, i: (b, i + n_skip // tm, 0)),
                  pl.BlockSpec((1, d), lambda b, i: (0, 0))],
        out_specs=pl.BlockSpec((None, tm, d), lambda b, i: (b, i, 0)),
        out_shape=jax.ShapeDtypeStruct((bsz, seq - n_skip, d), h.dtype),
        compiler_params=_params("parallel", "parallel"),
        name="final_rmsnorm",
    )(h, g.reshape(1, d))


def _matmul_kernel(x_ref, w_ref, o_ref):
    o_ref[...] = _dot(x_ref[...], w_ref[...]).astype(o_ref.dtype)


def _matmul(x, w, out_dtype, name):
    m, k = x.shape
    n = w.shape[1]
    tm = _pick(m, (1536, 1408, 1024, 768, 512, 256, 128))
    tn = _pick(n, (512, 256, 128))
    return pl.pallas_call(
        _matmul_kernel,
        grid=(m // tm, n // tn),
        in_specs=[pl.BlockSpec((tm, k), lambda i, j: (i, 0)),
                  pl.BlockSpec((k, tn), lambda i, j: (0, j))],
        out_specs=pl.BlockSpec((tm, tn), lambda i, j: (i, j)),
        out_shape=jax.ShapeDtypeStruct((m, n), out_dtype),
        compiler_params=_params("parallel", "parallel"),
        name=name,
    )(x, w)


def _out_proj_kernel(a1_ref, a2_ref, a3_ref, w1_ref, w2_ref, w3_ref, h_ref, o_ref):
    y = _dot(a1_ref[...], w1_ref[...])
    y = y + _dot(a2_ref[...], w2_ref[...])
    y = y + _dot(a3_ref[...], w3_ref[...])
    o_ref[...] = h_ref[...] + y


def _out_proj(a1, a2, a3, w1, w2, w3, h):
    m, d = h.shape
    tm = _pick(m, (768, 512, 256, 128))
    tn = _pick(d, (1024, 512, 256, 128))
    act = lambda a: pl.BlockSpec((tm, a.shape[1]), lambda i, j: (i, 0))
    wgt = lambda w: pl.BlockSpec((w.shape[0], tn), lambda i, j: (0, j))
    return pl.pallas_call(
        _out_proj_kernel,
        grid=(m // tm, d // tn),
        in_specs=[act(a1), act(a2), act(a3), wgt(w1), wgt(w2), wgt(w3),
                  pl.BlockSpec((tm, tn), lambda i, j: (i, j))],
        out_specs=pl.BlockSpec((tm, tn), lambda i, j: (i, j)),
        out_shape=jax.ShapeDtypeStruct((m, d), F32),
        compiler_params=_params("parallel", "parallel"),
        name="out_proj",
    )(a1, a2, a3, w1, w2, w3, h)


def _gla_kernel(q_ref, k_ref, v_ref, r_ref, s_ref, wg_ref, bg_ref, ng_ref, o_ref, st_ref,
                *, n_chunks):
    @pl.when(pl.program_id(1) == 0)
    def _():
        st_ref[...] = jnp.zeros_like(st_ref)

    row = lax.broadcasted_iota(jnp.int32, (CHUNK, CHUNK), 0)
    col = lax.broadcasted_iota(jnp.int32, (CHUNK, CHUNK), 1)
    causal = row >= col
    tri = jnp.where(causal, 1.0, 0.0).astype(BF16)
    scale = GLA_DK ** -0.5
    wg_parts = _split_bf16(wg_ref[...], 2)

    def chunk(c, carry):
        rows = pl.ds(pl.multiple_of(c * CHUNK, CHUNK), CHUNK)
        x = _dot_split2(_split_bf16(s_ref[rows, :], 2), wg_parts) + bg_ref[...]
        g = _log_sigmoid(x) * (1.0 / GLA_TAU)
        b = _dot_exact_by_f32(tri, g)
        b_last = b[CHUNK - 1:CHUNK, :]
        q = q_ref[rows, :].astype(F32)
        k = k_ref[rows, :].astype(F32)
        qt = (q * scale * jnp.exp(b)).astype(BF16)
        kt = (k * jnp.exp(-b)).astype(BF16)
        kl = (k * jnp.exp(b_last - b)).astype(BF16)
        dec = jnp.exp(b_last)
        heads = range(GLA_HEADS)
        ks = [slice(h * GLA_DK, (h + 1) * GLA_DK) for h in heads]
        vs = [slice(h * GLA_DV, (h + 1) * GLA_DV) for h in heads]
        att = [jnp.where(causal, _dot_nt(qt[:, ks[h]], kt[:, ks[h]]), 0.0).astype(BF16) for h in heads]
        st = [st_ref[h] for h in heads]
        vh = [v_ref[rows, vs[h]] for h in heads]
        o = [_dot(att[h], vh[h]) + _dot_nt(qt[:, ks[h]], st[h].astype(BF16)) for h in heads]
        for h in heads:
            st_ref[h] = st[h] * dec[:, ks[h]] + _dot_tn(vh[h], kl[:, ks[h]])
        for h in heads:
            gate = r_ref[rows, vs[h]].astype(F32)
            o_ref[rows, vs[h]] = _head_norm_gate(o[h], ng_ref[...], gate).astype(o_ref.dtype)
        return carry

    lax.fori_loop(0, n_chunks, chunk, 0)


def _gla(u_main, u_small, w_gate_pad, b_gate, norm_g):
    bsz, seq, _ = u_main.shape
    t = _pick(seq, (768, 384, 256, 128, 64))
    col = lambda off, w: pl.BlockSpec((None, t, w), lambda b, i: (b, i, off // w))
    full = lambda a: pl.BlockSpec(a.shape, lambda b, i: (0,) * a.ndim)
    return pl.pallas_call(
        functools.partial(_gla_kernel, n_chunks=t // CHUNK),
        grid=(bsz, seq // t),
        in_specs=[col(OFF_GQ, GLA_QK), col(OFF_GK, GLA_QK), col(OFF_GV, GLA_W), col(OFF_GR, GLA_W),
                  pl.BlockSpec((None, t, LANE), lambda b, i: (b, i, 0)),
                  full(w_gate_pad), full(b_gate), full(norm_g)],
        out_specs=pl.BlockSpec((None, t, GLA_W), lambda b, i: (b, i, 0)),
        out_shape=jax.ShapeDtypeStruct((bsz, seq, GLA_W), BF16),
        scratch_shapes=[pltpu.VMEM((GLA_HEADS, GLA_DV, GLA_DK), F32)],
        compiler_params=_params("parallel", "arbitrary"),
        name="gla",
    )(u_main, u_main, u_main, u_main, u_small, w_gate_pad, b_gate, norm_g)


GDN_PREP_T = 128
HALO = 16


def _gdn_prep_kernel(q_ref, k_ref, v_ref, qh_ref, kh_ref, vh_ref, s_ref, cw_ref, al_ref, dt_ref,
                     eb_ref, eg_ref, qg_ref, kd_ref, u_ref, w_ref, aqk_ref, gl_ref,
                     qs_ref, ks_ref, vs_ref, xe_ref):
    first = pl.program_id(1) == 0
    t = q_ref.shape[0]

    def conv_silu(x_ref, h_ref, xe_ref, w):
        xe_ref[0:SUBLANE, :] = jnp.where(first, 0.0, h_ref[...].astype(F32)[HALO - SUBLANE:, :])
        xe_ref[SUBLANE:, :] = x_ref[...].astype(F32)
        y = xe_ref[SUBLANE:, :] * w[CONV_K - 1:CONV_K, :]
        for j in range(1, CONV_K):
            y = y + xe_ref[SUBLANE - j:SUBLANE - j + t, :] * w[CONV_K - 1 - j:CONV_K - j, :]
        return y * _sigmoid(y)

    qc = conv_silu(q_ref, qh_ref, xe_ref.at[0], cw_ref[:, 0:GDN_QK])
    kc = conv_silu(k_ref, kh_ref, xe_ref.at[1], cw_ref[:, GDN_QK:2 * GDN_QK])
    vs_ref[...] = conv_silu(v_ref, vh_ref, xe_ref.at[2], cw_ref[:, 2 * GDN_QK:2 * GDN_QK + GDN_W])
    for h in range(GDN_HEADS):
        sl = slice(h * GDN_DK, (h + 1) * GDN_DK)
        qh = qc[:, sl]
        kh = kc[:, sl]
        qs_ref[:, sl] = qh * (lax.rsqrt(jnp.sum(qh * qh, axis=-1, keepdims=True) + EPS)
                              * GDN_DK ** -0.5)
        ks_ref[:, sl] = kh * lax.rsqrt(jnp.sum(kh * kh, axis=-1, keepdims=True) + EPS)

    s = s_ref[...]
    beta = _sigmoid(s)
    g = -jnp.exp(al_ref[...]) * _softplus(s + dt_ref[...])

    row = lax.broadcasted_iota(jnp.int32, (CHUNK, CHUNK), 0)
    col = lax.broadcasted_iota(jnp.int32, (CHUNK, CHUNK), 1)
    causal = row >= col
    strict = row > col
    tri = jnp.where(causal, 1.0, 0.0).astype(BF16)
    zeros_half = jnp.zeros((CHUNK, GDN_DV - CHUNK), F32)

    pairs = [(c, h) for c in range(t // CHUNK) for h in range(GDN_HEADS)]
    a_neg, rhs = {}, {}
    for c in range(t // CHUNK):
        rows = slice(c * CHUNK, (c + 1) * CHUNK)
        gcum = _dot_exact_by_f32(tri, g[rows])
        gcum_t = gcum.T
        gb = _dot_f32_by_exact(gcum, eg_ref[...])
        bb = _dot_f32_by_exact(beta[rows], eb_ref[...])
        g_last = gb[CHUNK - 1:CHUNK, :]
        eg = jnp.exp(gb)
        q = qs_ref[rows, :]
        k = ks_ref[rows, :]
        kb = k * bb
        qg_ref[rows, :] = (q * eg).astype(BF16)
        kd_ref[rows, :] = (k * jnp.exp(g_last - gb)).astype(BF16)
        gl_ref[c * SUBLANE:(c + 1) * SUBLANE, :] = jnp.broadcast_to(jnp.exp(g_last), (SUBLANE, GDN_W))
        rhs_v = vs_ref[rows, :] * bb
        rhs_k = kb * eg
        for h in range(GDN_HEADS):
            sl = slice(h * GDN_DK, (h + 1) * GDN_DK)
            lhs = jnp.concatenate([kb[:, sl], q[:, sl]], axis=0).astype(BF16)
            pr = _dot_nt(lhs, k[:, sl].astype(BF16))
            g_t = gb[:, h * GDN_DK:h * GDN_DK + CHUNK]
            g_s = gcum_t[S_A + h:S_A + h + 1, :]
            dec = jnp.exp(jnp.where(causal, g_t - g_s, -jnp.inf))
            a_neg[c, h] = jnp.where(strict, -pr[:CHUNK] * dec, 0.0)
            aqk_ref[rows, sl] = jnp.concatenate([pr[CHUNK:] * dec, zeros_half], axis=1).astype(BF16)
            rhs[c, h] = jnp.concatenate([rhs_v[:, sl], rhs_k[:, sl]], axis=1)
    toff = dict(a_neg)
    p = dict(a_neg)
    for _ in range(5):
        for key in pairs:
            pb = p[key].astype(BF16)
            p[key] = _dot(pb, pb)
        for key in pairs:
            toff[key] = toff[key] + p[key] + _dot(toff[key].astype(BF16), p[key].astype(BF16))
    for c, h in pairs:
        rows = slice(c * CHUNK, (c + 1) * CHUNK)
        sl = slice(h * GDN_DK, (h + 1) * GDN_DK)
        sol = rhs[c, h] + _dot(toff[c, h].astype(BF16), rhs[c, h].astype(BF16))
        u_ref[rows, sl] = sol[:, :GDN_DV]
        w_ref[rows, sl] = sol[:, GDN_DV:].astype(BF16)


def _gdn_scan_kernel(qg_ref, kd_ref, u_ref, w_ref, aqk_ref, gl_ref, z_ref, ng_ref, o_ref, st_ref,
                     *, n_chunks, n_heads):
    @pl.when(pl.program_id(2) == 0)
    def _():
        st_ref[...] = jnp.zeros_like(st_ref)

    def chunk(c, carry):
        rows = pl.ds(pl.multiple_of(c * CHUNK, CHUNK), CHUNK)
        grow = pl.ds(pl.multiple_of(c * SUBLANE, SUBLANE), SUBLANE)
        heads = range(n_heads)
        sls = [slice(h * GDN_DK, (h + 1) * GDN_DK) for h in heads]
        st = [st_ref[h] for h in heads]
        ws = [_dot(jnp.concatenate([w_ref[rows, sls[h]], qg_ref[rows, sls[h]]], axis=0),
                   st[h].astype(BF16)) for h in heads]
        vn = [(u_ref[rows, sls[h]] - ws[h][:CHUNK]).astype(BF16) for h in heads]
        o = [ws[h][CHUNK:] + _dot(aqk_ref[rows, h * GDN_DK:h * GDN_DK + CHUNK], vn[h]) for h in heads]
        for h in heads:
            st_ref[h] = st[h] * gl_ref[grow, sls[h]][0:1, :] + _dot_tn(kd_ref[rows, sls[h]], vn[h])
        for h in heads:
            gate = z_ref[rows, sls[h]].astype(F32)
            o_ref[rows, sls[h]] = _head_norm_gate(o[h], ng_ref[...], gate).astype(o_ref.dtype)
        return carry

    lax.fori_loop(0, n_chunks, chunk, 0)


def _gdn(u_main, u_small, conv_w, a_log_row, dt_row, e_beta, e_g, norm_g):
    bsz, seq, _ = u_main.shape
    t = GDN_PREP_T
    assert seq % t == 0 and t % HALO == 0
    blk = lambda off: pl.BlockSpec((None, t, GDN_W), lambda b, i: (b, i, off // GDN_W))
    halo = lambda off: pl.BlockSpec(
        (None, HALO, GDN_W), lambda b, i: (b, jnp.maximum(i * (t // HALO) - 1, 0), off // GDN_W))
    full = lambda a: pl.BlockSpec(a.shape, lambda b, i: (0,) * a.ndim)
    out = lambda: pl.BlockSpec((None, t, GDN_W), lambda b, i: (b, i, 0))
    gl_rows = t // CHUNK * SUBLANE
    sds = lambda dt: jax.ShapeDtypeStruct((bsz, seq, GDN_W), dt)
    qg, kd, u, w, aqk, gl = pl.pallas_call(
        _gdn_prep_kernel,
        grid=(bsz, seq // t),
        in_specs=[blk(OFF_DQ), blk(OFF_DK), blk(OFF_DV), halo(OFF_DQ), halo(OFF_DK), halo(OFF_DV),
                  pl.BlockSpec((None, t, LANE), lambda b, i: (b, i, 0)),
                  full(conv_w), full(a_log_row), full(dt_row), full(e_beta), full(e_g)],
        out_specs=[out(), out(), out(), out(), out(),
                   pl.BlockSpec((None, gl_rows, GDN_W), lambda b, i: (b, i, 0))],
        out_shape=[sds(BF16), sds(BF16), sds(F32), sds(BF16), sds(BF16),
                   jax.ShapeDtypeStruct((bsz, seq // CHUNK * SUBLANE, GDN_W), F32)],
        scratch_shapes=[pltpu.VMEM((t, GDN_W), F32)] * 3 + [pltpu.VMEM((3, t + SUBLANE, GDN_W), F32)],
        compiler_params=_params("parallel", "parallel"),
        name="gdn_prep",
    )(u_main, u_main, u_main, u_main, u_main, u_main, u_small, conv_w, a_log_row, dt_row, e_beta, e_g)

    ts = _pick(seq, (384, 256, 128, 64))
    hb = GDN_HEADS
    wid = hb * GDN_DK
    blk = lambda: pl.BlockSpec((None, ts, wid), lambda b, hg, i: (b, i, hg))
    return pl.pallas_call(
        functools.partial(_gdn_scan_kernel, n_chunks=ts // CHUNK, n_heads=hb),
        grid=(bsz, GDN_HEADS // hb, seq // ts),
        in_specs=[blk(), blk(), blk(), blk(), blk(),
                  pl.BlockSpec((None, ts // CHUNK * SUBLANE, wid), lambda b, hg, i: (b, i, hg)),
                  pl.BlockSpec((None, ts, wid), lambda b, hg, i: (b, i, OFF_DZ // wid + hg)),
                  pl.BlockSpec(norm_g.shape, lambda b, hg, i: (0, 0))],
        out_specs=blk(),
        out_shape=jax.ShapeDtypeStruct((bsz, seq, GDN_W), BF16),
        scratch_shapes=[pltpu.VMEM((hb, GDN_DK, GDN_DV), F32)],
        compiler_params=_params("parallel", "parallel", "arbitrary"),
        name="gdn_scan",
    )(qg, kd, u, w, aqk, gl, u_main, norm_g)


SB_RB = 128
SB_LOOK = 2
SB_QT = 2 * SB_RB
SB_EXP_ZERO = 104.0


def _sb_kernel(q_ref, k_ref, v_ref, g_ref, ng_ref, o_ref, acc_ref, run_ref, *, n_invalid):
    seq = q_ref.shape[0]
    rb = SB_RB
    scale = SB_D ** -0.5
    jr = lax.broadcasted_iota(jnp.int32, (2 * rb, 2 * rb), 0)
    sc = lax.broadcasted_iota(jnp.int32, (2 * rb, 2 * rb), 1)
    rr = jnp.where((sc >= rb) | ((jr & (rb - 1)) >= sc), 1.0, 0.0).astype(BF16)
    row_i = lax.broadcasted_iota(jnp.int32, (rb, rb), 0)
    col_i = lax.broadcasted_iota(jnp.int32, (rb, rb), 1)
    below_diag = col_i < row_i

    def softplus(z):
        return jnp.maximum(z, 0.0) + jnp.log(1.0 + jnp.exp2(jnp.abs(z) * -LOG2E))

    def weights(z, sp, run):
        hi = sp.astype(BF16)
        lo = (sp - hi.astype(F32)).astype(BF16)
        cs = _dot(jnp.concatenate([hi, lo], axis=1), rr)
        return jnp.exp(z - cs[:, :rb] - run).astype(BF16), cs[:, rb:]

    def mask(z, sp, vis):
        return jnp.where(vis, z, -jnp.inf), jnp.where(vis, sp, 0.0)

    def finish(rows, acc):
        gate = g_ref[rows, :].astype(F32)
        o_ref[rows, :] = _head_norm_gate(acc, ng_ref[...], gate).astype(o_ref.dtype)

    def row_block(zs, key0, check_valid):
        run = jnp.zeros((rb, rb), F32)
        parts = [None] * len(zs)
        for c in reversed(range(len(zs))):
            z, sp = zs[c], softplus(zs[c])
            vis = below_diag if c == len(zs) - 1 else None
            if check_valid:
                ok = (key0 + c * rb + col_i) >= n_invalid
                vis = ok if vis is None else (vis & ok)
            if vis is not None:
                z, sp = mask(z, sp, vis)
            parts[c], tot = weights(z, sp, run)
            run = run + tot
        return jnp.concatenate(parts, axis=1), run

    def qtile(r0, looks, check_valid):
        k0 = r0 - looks[0] * rb
        nk = looks[0] + 2
        zz = _dot_nt(q_ref[pl.ds(r0, SB_QT), :], k_ref[pl.ds(k0, nk * rb), :]) * scale
        run_min = None
        for r in range(2):
            c0 = r - looks[r] + looks[0]
            zs = [zz[r * rb:(r + 1) * rb, c * rb:(c + 1) * rb] for c in range(c0, c0 + looks[r] + 1)]
            a, run = row_block(zs, k0 + c0 * rb, check_valid)
            acc = _dot(a, v_ref[pl.ds(k0 + c0 * rb, (looks[r] + 1) * rb), :])
            rows = pl.ds(r0 + r * rb, rb)
            acc_ref[rows, :] = acc
            run_ref[rows, :] = run
            finish(rows, acc)
            run_min = run if run_min is None else jnp.minimum(run_min, run)
        return run_min

    first_open = n_invalid // rb + SB_LOOK + 1
    assert first_open % 2 == 0 and first_open * rb <= seq and (first_open - SB_LOOK) * rb >= n_invalid
    qtile(0, (0, 1), True)
    for qi in range(1, first_open // 2):
        qtile(qi * SB_QT, (SB_LOOK, SB_LOOK), True)

    def first_pass(qi, run_min):
        r0 = pl.multiple_of(qi * SB_QT, SB_QT)
        return jnp.minimum(run_min, qtile(r0, (SB_LOOK, SB_LOOK), False))

    run_min = lax.fori_loop(first_open // 2, seq // SB_QT, first_pass,
                            jnp.full((rb, rb), 2 * SB_EXP_ZERO, F32), unroll=4)

    @pl.when(jnp.min(run_min) < SB_EXP_ZERO)
    def _():
        def second_pass(b, carry):
            rows = pl.ds(pl.multiple_of(b * rb, rb), rb)

            def unfinished(state):
                j, lowest = state
                return (j >= 0) & (lowest < SB_EXP_ZERO)

            def sub_block(state):
                j, _ = state
                keys = pl.ds(pl.multiple_of(j * rb, rb), rb)
                z = _dot_nt(q_ref[rows, :], k_ref[keys, :]) * scale
                z, sp = mask(z, softplus(z), (j * rb + col_i) >= n_invalid)
                run = run_ref[rows, :]
                a, tot = weights(z, sp, run)
                acc_ref[rows, :] += _dot(a, v_ref[keys, :])
                run_ref[rows, :] = run + tot
                return j - 1, jnp.min(run + tot)

            lax.while_loop(unfinished, sub_block, (b - SB_LOOK - 1, jnp.min(run_ref[rows, :])))
            finish(rows, acc_ref[rows, :])
            return carry

        lax.fori_loop(first_open, seq // rb, second_pass, 0)


def _sb(u_main, norm_g, n_invalid):
    bsz, seq, _ = u_main.shape
    assert seq % SB_QT == 0 and SB_D == SB_RB
    blk = lambda off: pl.BlockSpec((None, seq, SB_D), lambda b, h: (b, 0, off // SB_D + h))
    return pl.pallas_call(
        functools.partial(_sb_kernel, n_invalid=n_invalid),
        grid=(bsz, SB_HEADS),
        in_specs=[blk(OFF_SQ), blk(OFF_SK), blk(OFF_SV), blk(OFF_SG),
                  pl.BlockSpec(norm_g.shape, lambda b, h: (0, 0))],
        out_specs=pl.BlockSpec((None, seq, SB_D), lambda b, h: (b, 0, h)),
        out_shape=jax.ShapeDtypeStruct((bsz, seq, SB_W), BF16),
        scratch_shapes=[pltpu.VMEM((seq, SB_D), F32), pltpu.VMEM((seq, SB_D), F32)],
        compiler_params=_params("parallel", "parallel"),
        name="stick_breaking",
    )(u_main, u_main, u_main, u_main, norm_g)


def _place(vec, lane0):
    return jnp.zeros((1, LANE), F32).at[0, lane0:lane0 + vec.shape[0]].set(vec.astype(F32))


def _expand(lane0, heads, width):
    src = jnp.arange(LANE)[:, None]
    dst_head = jnp.arange(heads * width)[None, :] // width
    return (src == lane0 + dst_head).astype(BF16)


def _layer(h, xn, w_in, gla_w_gate, gla_b_gate, gla_norm_g, gdn_conv_w, gdn_a_log, gdn_dt_bias,
           gdn_norm_g, sb_norm_g, w_out, bsz, seq):
    m = bsz * seq
    o = _IN_OFF
    w_small = jnp.concatenate(
        [w_in[:, o[4]:o[5]], w_in[:, o[9]:o[11]],
         jnp.zeros((w_in.shape[0], LANE - GLA_RANK - 2 * GDN_HEADS), w_in.dtype)], axis=1).astype(BF16)

    def project(lo, hi, name):
        u = _matmul(xn, w_in[:, lo:hi].astype(BF16), BF16, name)
        return u.reshape(bsz, seq, hi - lo)

    u_gla = project(o[0], o[4], "in_proj_gla")
    u_gdn = project(o[5], o[9], "in_proj_gdn")
    u_sb = project(o[11], o[15], "in_proj_sb")
    u_small = _matmul(xn, w_small, F32, "in_proj_small").reshape(bsz, seq, LANE)

    w_gate_pad = jnp.zeros((LANE, GLA_QK), F32).at[S_LR:S_LR + GLA_RANK].set(gla_w_gate)
    o_gla = _gla(u_gla, u_small, w_gate_pad, gla_b_gate.reshape(1, GLA_QK),
                 gla_norm_g.reshape(1, GLA_DV))
    o_gdn = _gdn(u_gdn, u_small, gdn_conv_w, _place(gdn_a_log, S_A), _place(gdn_dt_bias, S_A),
                 _expand(S_B, GDN_HEADS, GDN_DK), _expand(S_A, GDN_HEADS, GDN_DK),
                 gdn_norm_g.reshape(1, GDN_DV))
    o_sb = _sb(u_sb, sb_norm_g.reshape(1, SB_D), PREFIX - N_META)

    w1, w2, w3 = (w_out[lo:hi].astype(BF16) for lo, hi in
                  ((0, GLA_W), (GLA_W, GLA_W + GDN_W), (GLA_W + GDN_W, GLA_W + GDN_W + SB_W)))
    return _out_proj(o_gla.reshape(m, GLA_W), o_gdn.reshape(m, GDN_W), o_sb.reshape(m, SB_W),
                     w1, w2, w3, h)


def kernel(x, meta, norm_g, w_in, gla_w_gate, gla_b_gate, gla_norm_g, gdn_conv_w, gdn_a_log,
           gdn_dt_bias, gdn_norm_g, sb_norm_g, w_out, final_g):
    bsz, n, d = x.shape
    prefix = jnp.concatenate([jnp.zeros((PREFIX - N_META, d), x.dtype), meta.astype(x.dtype)], axis=0)
    seq = PREFIX + n
    h, xn = _embed_norm(x, prefix, norm_g[0])
    h = h.reshape(bsz * seq, d)
    xn = xn.reshape(bsz * seq, d)
    for l in range(norm_g.shape[0]):
        if l > 0:
            xn = _rmsnorm(h, norm_g[l], BF16)
        h = _layer(h, xn, w_in[l], gla_w_gate[l], gla_b_gate[l], gla_norm_g[l], gdn_conv_w[l],
                   gdn_a_log[l], gdn_dt_bias[l], gdn_norm_g[l], sb_norm_g[l], w_out[l], bsz, seq)
    return _final_rmsnorm(h.reshape(bsz, seq, d), final_g, PREFIX)
```

```python
import functools

import jax
import jax.numpy as jnp
from jax import lax
from jax.experimental import pallas as pl
from jax.experimental.pallas import tpu as pltpu

F32 = jnp.float32
BF16 = jnp.bfloat16
HIGHEST = lax.Precision.HIGHEST

N_META = 16
PREFIX = 256
CHUNK = 64
EPS = 1e-6
LOG2E = 1.4426950408889634

GLA_HEADS, GLA_DK, GLA_DV, GLA_RANK, GLA_TAU = 4, 128, 256, 16, 16.0
GDN_HEADS, GDN_DK, GDN_DV, CONV_K = 12, 128, 128, 4
SB_HEADS, SB_D = 12, 128

GLA_QK = GLA_HEADS * GLA_DK
GLA_W = GLA_HEADS * GLA_DV
GDN_QK = GDN_HEADS * GDN_DK
GDN_W = GDN_HEADS * GDN_DV
SB_W = SB_HEADS * SB_D

_IN_SPLITS = (GLA_QK, GLA_QK, GLA_W, GLA_W, GLA_RANK,
              GDN_QK, GDN_QK, GDN_W, GDN_W, GDN_HEADS, GDN_HEADS,
              SB_W, SB_W, SB_W, SB_W)
_IN_OFF = [0]
for _w in _IN_SPLITS:
    _IN_OFF.append(_IN_OFF[-1] + _w)

OFF_GQ, OFF_GK, OFF_GV, OFF_GR = 0, 512, 1024, 2048
OFF_DQ, OFF_DK, OFF_DV, OFF_DZ = 0, 1536, 3072, 4608
OFF_SQ, OFF_SK, OFF_SV, OFF_SG = 0, 1536, 3072, 4608
LANE = 128
SUBLANE = 8
S_LR, S_B, S_A = 0, GLA_RANK, GLA_RANK + GDN_HEADS

V7X_VMEM_LIMIT_BYTES = 56 * 1024 * 1024


def _pick(n, candidates):
    for c in candidates:
        if n % c == 0:
            return c
    raise ValueError(f"no block size in {candidates} divides {n}")


def _params(*sem):
    return pltpu.CompilerParams(dimension_semantics=sem, vmem_limit_bytes=V7X_VMEM_LIMIT_BYTES)


def _dot(a, b):
    return jnp.dot(a, b, preferred_element_type=F32)


def _dot_nt(a, b):
    return lax.dot_general(a, b, (((1,), (1,)), ((), ())), preferred_element_type=F32)


def _dot_tn(a, b):
    return lax.dot_general(a, b, (((0,), (0,)), ((), ())), preferred_element_type=F32)


def _split_bf16(x, n):
    parts = []
    for _ in range(n):
        p = x.astype(BF16)
        parts.append(p)
        x = x - p.astype(F32)
    return parts


def _dot_f32_by_exact(a, b_exact):
    hi, mid, lo = _split_bf16(a, 3)
    return _dot(hi, b_exact) + _dot(mid, b_exact) + _dot(lo, b_exact)


def _dot_exact_by_f32(a_exact, b):
    hi, mid, lo = _split_bf16(b, 3)
    return _dot(a_exact, hi) + _dot(a_exact, mid) + _dot(a_exact, lo)


def _dot_split2(a_parts, b_parts):
    return _dot(a_parts[0], b_parts[0]) + _dot(a_parts[0], b_parts[1]) + _dot(a_parts[1], b_parts[0])


def _softplus(x):
    return jnp.maximum(x, 0.0) + jnp.log(1.0 + jnp.exp(-jnp.abs(x)))


def _log_sigmoid(x):
    return jnp.minimum(x, 0.0) - jnp.log(1.0 + jnp.exp(-jnp.abs(x)))


def _sigmoid(x):
    return 1.0 / (1.0 + jnp.exp(-x))


def _head_norm_gate(o, norm_g, gate):
    ms = jnp.mean(o * o, axis=-1, keepdims=True)
    return o * lax.rsqrt(ms + EPS) * norm_g * (gate * _sigmoid(gate))


def _rmsnorm_kernel(x_ref, g_ref, o_ref):
    x = x_ref[...]
    ms = jnp.mean(x * x, axis=-1, keepdims=True)
    o_ref[...] = (x * lax.rsqrt(ms + EPS) * g_ref[...]).astype(o_ref.dtype)


def _rmsnorm(x, g, out_dtype):
    m, d = x.shape
    tm = _pick(m, (512, 256, 128, 64, 8))
    return pl.pallas_call(
        _rmsnorm_kernel,
        grid=(m // tm,),
        in_specs=[pl.BlockSpec((tm, d), lambda i: (i, 0)),
                  pl.BlockSpec((1, d), lambda i: (0, 0))],
        out_specs=pl.BlockSpec((tm, d), lambda i: (i, 0)),
        out_shape=jax.ShapeDtypeStruct((m, d), out_dtype),
        compiler_params=_params("parallel"),
        name="rmsnorm",
    )(x, g.reshape(1, d))


def _embed_norm_kernel(x_ref, p_ref, g_ref, h_ref, o_ref):
    x = jnp.where(pl.program_id(1) == 0, p_ref[...], x_ref[...])
    h_ref[...] = x
    ms = jnp.mean(x * x, axis=-1, keepdims=True)
    o_ref[...] = (x * lax.rsqrt(ms + EPS) * g_ref[...]).astype(o_ref.dtype)


def _embed_norm(x, prefix, g):
    bsz, n, d = x.shape
    tm = prefix.shape[0]
    assert n % tm == 0
    blk = lambda: pl.BlockSpec((None, tm, d), lambda b, i: (b, i, 0))
    return pl.pallas_call(
        _embed_norm_kernel,
        grid=(bsz, n // tm + 1),
        in_specs=[pl.BlockSpec((None, tm, d), lambda b, i: (b, jnp.maximum(i - 1, 0), 0)),
                  pl.BlockSpec((tm, d), lambda b, i: (0, 0)),
                  pl.BlockSpec((1, d), lambda b, i: (0, 0))],
        out_specs=[blk(), blk()],
        out_shape=[jax.ShapeDtypeStruct((bsz, n + tm, d), x.dtype),
                   jax.ShapeDtypeStruct((bsz, n + tm, d), BF16)],
        compiler_params=_params("parallel", "parallel"),
        name="embed_norm",
    )(x, prefix, g.reshape(1, d))


def _final_rmsnorm(h, g, n_skip):
    bsz, seq, d = h.shape
    tm = _pick(n_skip, (256, 128, 64, 8))
    assert (seq - n_skip) % tm == 0
    return pl.pallas_call(
        _rmsnorm_kernel,
        grid=(bsz, (seq - n_skip) // tm),
        in_specs=[pl.BlockSpec((None, tm, d), lambda b, i: (b, i + n_skip // tm, 0)),
                  pl.BlockSpec((1, d), lambda b, i: (0, 0))],
        out_specs=pl.BlockSpec((None, tm, d), lambda b, i: (b, i, 0)),
        out_shape=jax.ShapeDtypeStruct((bsz, seq - n_skip, d), h.dtype),
        compiler_params=_params("parallel", "parallel"),
        name="final_rmsnorm",
    )(h, g.reshape(1, d))


def _matmul_kernel(x_ref, w_ref, o_ref):
    o_ref[...] = _dot(x_ref[...], w_ref[...]).astype(o_ref.dtype)


def _matmul(x, w, out_dtype, name):
    m, k = x.shape
    n = w.shape[1]
    tm = _pick(m, (1536, 1408, 1024, 768, 512, 256, 128))
    tn = _pick(n, (512, 256, 128))
    return pl.pallas_call(
        _matmul_kernel,
        grid=(m // tm, n // tn),
        in_specs=[pl.BlockSpec((tm, k), lambda i, j: (i, 0)),
                  pl.BlockSpec((k, tn), lambda i, j: (0, j))],
        out_specs=pl.BlockSpec((tm, tn), lambda i, j: (i, j)),
        out_shape=jax.ShapeDtypeStruct((m, n), out_dtype),
        compiler_params=_params("parallel", "parallel"),
        name=name,
    )(x, w)


PROJ_TN = 512
CAST_ROWS = 512


def _cast_weight_tile(dst_ref, src_ref):
    step = min(CAST_ROWS, dst_ref.shape[0])

    def chunk(r, carry):
        rows = pl.ds(pl.multiple_of(r * step, step), step)
        dst_ref[rows, :] = src_ref[rows, :].astype(BF16)
        return carry

    lax.fori_loop(0, dst_ref.shape[0] // step, chunk, 0)


def _in_proj_kernel(x_ref, wt_ref, o_ref, w_ref):
    @pl.when(pl.program_id(1) == 0)
    def _():
        for c in range(wt_ref.shape[0] // LANE):
            cols = slice(c * LANE, (c + 1) * LANE)
            w_ref[:, cols] = wt_ref[cols, :].T.astype(BF16)

    o_ref[...] = _dot(x_ref[...], w_ref[...]).astype(o_ref.dtype)


def _in_proj(xn, w_t, row0, width, out_dtype, name):
    m, k = xn.shape
    tn = min(PROJ_TN, width)
    assert width % tn == 0 and row0 % SUBLANE == 0 and tn % LANE == 0
    tm = _pick(m, (1536, 1056, 768, 512, 256, 128))
    return pl.pallas_call(
        _in_proj_kernel,
        grid=(width // tn, m // tm),
        in_specs=[pl.BlockSpec((tm, k), lambda j, i: (i, 0)),
                  pl.BlockSpec((pl.Element(tn), pl.Element(k)),
                               lambda j, i: (pl.multiple_of(row0 + j * tn, SUBLANE), 0),
                               pipeline_mode=pl.Buffered(1))],
        out_specs=pl.BlockSpec((tm, tn), lambda j, i: (i, j)),
        out_shape=jax.ShapeDtypeStruct((m, width), out_dtype),
        scratch_shapes=[pltpu.VMEM((k, tn), BF16)],
        compiler_params=_params("arbitrary", "arbitrary"),
        name=name,
    )(xn, w_t)


def _out_proj_kernel(a1_ref, a2_ref, a3_ref, wf_ref, h_ref, o_ref, w_ref):
    @pl.when(pl.program_id(1) == 0)
    def _():
        _cast_weight_tile(w_ref, wf_ref)

    k1, k2 = a1_ref.shape[1], a2_ref.shape[1]
    y = _dot(a1_ref[...], w_ref[0:k1, :])
    y = y + _dot(a2_ref[...], w_ref[k1:k1 + k2, :])
    y = y + _dot(a3_ref[...], w_ref[k1 + k2:, :])
    o_ref[...] = h_ref[...] + y


def _out_proj(a1, a2, a3, w_out, layer, h):
    m, d = h.shape
    k = w_out.shape[1]
    tn = PROJ_TN
    assert a1.shape[1] + a2.shape[1] + a3.shape[1] == k and d % tn == 0 and k % CAST_ROWS == 0
    tm = _pick(m, (1056, 768, 512, 256, 128))
    act = lambda a: pl.BlockSpec((tm, a.shape[1]), lambda j, i: (i, 0))
    return pl.pallas_call(
        _out_proj_kernel,
        grid=(d // tn, m // tm),
        in_specs=[act(a1), act(a2), act(a3),
                  pl.BlockSpec((None, k, tn), lambda j, i: (layer, 0, j)),
                  pl.BlockSpec((tm, tn), lambda j, i: (i, j))],
        out_specs=pl.BlockSpec((tm, tn), lambda j, i: (i, j)),
        out_shape=jax.ShapeDtypeStruct((m, d), F32),
        scratch_shapes=[pltpu.VMEM((k, tn), BF16)],
        compiler_params=_params("arbitrary", "arbitrary"),
        name="out_proj",
    )(a1, a2, a3, w_out, h)


def _gla_kernel(q_ref, k_ref, v_ref, r_ref, s_ref, wg_ref, bg_ref, ng_ref, o_ref, st_ref,
                *, n_chunks):
    @pl.when(pl.program_id(1) == 0)
    def _():
        st_ref[...] = jnp.zeros_like(st_ref)

    row = lax.broadcasted_iota(jnp.int32, (CHUNK, CHUNK), 0)
    col = lax.broadcasted_iota(jnp.int32, (CHUNK, CHUNK), 1)
    causal = row >= col
    tri = jnp.where(causal, 1.0, 0.0).astype(BF16)
    scale = GLA_DK ** -0.5
    wg_parts = _split_bf16(wg_ref[...], 2)

    def chunk(c, carry):
        rows = pl.ds(pl.multiple_of(c * CHUNK, CHUNK), CHUNK)
        x = _dot_split2(_split_bf16(s_ref[rows, :], 2), wg_parts) + bg_ref[...]
        g = _log_sigmoid(x) * (1.0 / GLA_TAU)
        b = _dot_exact_by_f32(tri, g)
        b_last = b[CHUNK - 1:CHUNK, :]
        q = q_ref[rows, :].astype(F32)
        k = k_ref[rows, :].astype(F32)
        qt = (q * scale * jnp.exp(b)).astype(BF16)
        kt = (k * jnp.exp(-b)).astype(BF16)
        kl = (k * jnp.exp(b_last - b)).astype(BF16)
        dec = jnp.exp(b_last)
        heads = range(GLA_HEADS)
        ks = [slice(h * GLA_DK, (h + 1) * GLA_DK) for h in heads]
        vs = [slice(h * GLA_DV, (h + 1) * GLA_DV) for h in heads]
        att = [jnp.where(causal, _dot_nt(qt[:, ks[h]], kt[:, ks[h]]), 0.0).astype(BF16) for h in heads]
        st = [st_ref[h] for h in heads]
        vh = [v_ref[rows, vs[h]] for h in heads]
        o = [_dot(att[h], vh[h]) + _dot_nt(qt[:, ks[h]], st[h].astype(BF16)) for h in heads]
        for h in heads:
            st_ref[h] = st[h] * dec[:, ks[h]] + _dot_tn(vh[h], kl[:, ks[h]])
        for h in heads:
            gate = r_ref[rows, vs[h]].astype(F32)
            o_ref[rows, vs[h]] = _head_norm_gate(o[h], ng_ref[...], gate).astype(o_ref.dtype)
        return carry

    lax.fori_loop(0, n_chunks, chunk, 0)


def _gla(u_main, u_small, w_gate_pad, b_gate, norm_g):
    bsz, seq, _ = u_main.shape
    t = _pick(seq, (768, 384, 256, 128, 64))
    col = lambda off, w: pl.BlockSpec((None, t, w), lambda b, i: (b, i, off // w))
    full = lambda a: pl.BlockSpec(a.shape, lambda b, i: (0,) * a.ndim)
    return pl.pallas_call(
        functools.partial(_gla_kernel, n_chunks=t // CHUNK),
        grid=(bsz, seq // t),
        in_specs=[col(OFF_GQ, GLA_QK), col(OFF_GK, GLA_QK), col(OFF_GV, GLA_W), col(OFF_GR, GLA_W),
                  pl.BlockSpec((None, t, LANE), lambda b, i: (b, i, 0)),
                  full(w_gate_pad), full(b_gate), full(norm_g)],
        out_specs=pl.BlockSpec((None, t, GLA_W), lambda b, i: (b, i, 0)),
        out_shape=jax.ShapeDtypeStruct((bsz, seq, GLA_W), BF16),
        scratch_shapes=[pltpu.VMEM((GLA_HEADS, GLA_DV, GLA_DK), F32)],
        compiler_params=_params("parallel", "arbitrary"),
        name="gla",
    )(u_main, u_main, u_main, u_main, u_small, w_gate_pad, b_gate, norm_g)


GDN_PREP_T = 128
HALO = 16


def _gdn_prep_kernel(q_ref, k_ref, v_ref, qh_ref, kh_ref, vh_ref, s_ref, cw_ref, al_ref, dt_ref,
                     eb_ref, eg_ref, qg_ref, kd_ref, u_ref, w_ref, aqk_ref, gl_ref,
                     qs_ref, ks_ref, vs_ref, xe_ref):
    first = pl.program_id(1) == 0
    t = q_ref.shape[0]

    def conv_silu(x_ref, h_ref, xe_ref, w):
        xe_ref[0:SUBLANE, :] = jnp.where(first, 0.0, h_ref[...].astype(F32)[HALO - SUBLANE:, :])
        xe_ref[SUBLANE:, :] = x_ref[...].astype(F32)
        y = xe_ref[SUBLANE:, :] * w[CONV_K - 1:CONV_K, :]
        for j in range(1, CONV_K):
            y = y + xe_ref[SUBLANE - j:SUBLANE - j + t, :] * w[CONV_K - 1 - j:CONV_K - j, :]
        return y * _sigmoid(y)

    qc = conv_silu(q_ref, qh_ref, xe_ref.at[0], cw_ref[:, 0:GDN_QK])
    kc = conv_silu(k_ref, kh_ref, xe_ref.at[1], cw_ref[:, GDN_QK:2 * GDN_QK])
    vs_ref[...] = conv_silu(v_ref, vh_ref, xe_ref.at[2], cw_ref[:, 2 * GDN_QK:2 * GDN_QK + GDN_W])
    for h in range(GDN_HEADS):
        sl = slice(h * GDN_DK, (h + 1) * GDN_DK)
        qh = qc[:, sl]
        kh = kc[:, sl]
        qs_ref[:, sl] = qh * (lax.rsqrt(jnp.sum(qh * qh, axis=-1, keepdims=True) + EPS)
                              * GDN_DK ** -0.5)
        ks_ref[:, sl] = kh * lax.rsqrt(jnp.sum(kh * kh, axis=-1, keepdims=True) + EPS)

    s = s_ref[...]
    beta = _sigmoid(s)
    g = -jnp.exp(al_ref[...]) * _softplus(s + dt_ref[...])

    row = lax.broadcasted_iota(jnp.int32, (CHUNK, CHUNK), 0)
    col = lax.broadcasted_iota(jnp.int32, (CHUNK, CHUNK), 1)
    causal = row >= col
    strict = row > col
    tri = jnp.where(causal, 1.0, 0.0).astype(BF16)
    zeros_half = jnp.zeros((CHUNK, GDN_DV - CHUNK), F32)

    pairs = [(c, h) for c in range(t // CHUNK) for h in range(GDN_HEADS)]
    a_neg, rhs = {}, {}
    for c in range(t // CHUNK):
        rows = slice(c * CHUNK, (c + 1) * CHUNK)
        gcum = _dot_exact_by_f32(tri, g[rows])
        gcum_t = gcum.T
        gb = _dot_f32_by_exact(gcum, eg_ref[...])
        bb = _dot_f32_by_exact(beta[rows], eb_ref[...])
        g_last = gb[CHUNK - 1:CHUNK, :]
        eg = jnp.exp(gb)
        q = qs_ref[rows, :]
        k = ks_ref[rows, :]
        kb = k * bb
        qg_ref[rows, :] = (q * eg).astype(BF16)
        kd_ref[rows, :] = (k * jnp.exp(g_last - gb)).astype(BF16)
        gl_ref[c * SUBLANE:(c + 1) * SUBLANE, :] = jnp.broadcast_to(jnp.exp(g_last), (SUBLANE, GDN_W))
        rhs_v = vs_ref[rows, :] * bb
        rhs_k = kb * eg
        for h in range(GDN_HEADS):
            sl = slice(h * GDN_DK, (h + 1) * GDN_DK)
            lhs = jnp.concatenate([kb[:, sl], q[:, sl]], axis=0).astype(BF16)
            pr = _dot_nt(lhs, k[:, sl].astype(BF16))
            g_t = gb[:, h * GDN_DK:h * GDN_DK + CHUNK]
            g_s = gcum_t[S_A + h:S_A + h + 1, :]
            dec = jnp.exp(jnp.where(causal, g_t - g_s, -jnp.inf))
            a_neg[c, h] = jnp.where(strict, -pr[:CHUNK] * dec, 0.0)
            aqk_ref[rows, sl] = jnp.concatenate([pr[CHUNK:] * dec, zeros_half], axis=1).astype(BF16)
            rhs[c, h] = jnp.concatenate([rhs_v[:, sl], rhs_k[:, sl]], axis=1)
    toff = dict(a_neg)
    p = dict(a_neg)
    for _ in range(5):
        for key in pairs:
            pb = p[key].astype(BF16)
            p[key] = _dot(pb, pb)
        for key in pairs:
            toff[key] = toff[key] + p[key] + _dot(toff[key].astype(BF16), p[key].astype(BF16))
    for c, h in pairs:
        rows = slice(c * CHUNK, (c + 1) * CHUNK)
        sl = slice(h * GDN_DK, (h + 1) * GDN_DK)
        sol = rhs[c, h] + _dot(toff[c, h].astype(BF16), rhs[c, h].astype(BF16))
        u_ref[rows, sl] = sol[:, :GDN_DV]
        w_ref[rows, sl] = sol[:, GDN_DV:].astype(BF16)


def _gdn_scan_kernel(qg_ref, kd_ref, u_ref, w_ref, aqk_ref, gl_ref, z_ref, ng_ref, o_ref, st_ref,
                     *, n_chunks, n_heads):
    @pl.when(pl.program_id(2) == 0)
    def _():
        st_ref[...] = jnp.zeros_like(st_ref)

    def chunk(c, carry):
        rows = pl.ds(pl.multiple_of(c * CHUNK, CHUNK), CHUNK)
        grow = pl.ds(pl.multiple_of(c * SUBLANE, SUBLANE), SUBLANE)
        heads = range(n_heads)
        sls = [slice(h * GDN_DK, (h + 1) * GDN_DK) for h in heads]
        st = [st_ref[h] for h in heads]
        ws = [_dot(jnp.concatenate([w_ref[rows, sls[h]], qg_ref[rows, sls[h]]], axis=0),
                   st[h].astype(BF16)) for h in heads]
        vn = [(u_ref[rows, sls[h]] - ws[h][:CHUNK]).astype(BF16) for h in heads]
        o = [ws[h][CHUNK:] + _dot(aqk_ref[rows, h * GDN_DK:h * GDN_DK + CHUNK], vn[h]) for h in heads]
        for h in heads:
            st_ref[h] = st[h] * gl_ref[grow, sls[h]][0:1, :] + _dot_tn(kd_ref[rows, sls[h]], vn[h])
        for h in heads:
            gate = z_ref[rows, sls[h]].astype(F32)
            o_ref[rows, sls[h]] = _head_norm_gate(o[h], ng_ref[...], gate).astype(o_ref.dtype)
        return carry

    lax.fori_loop(0, n_chunks, chunk, 0)


def _gdn(u_main, u_small, conv_w, a_log_row, dt_row, e_beta, e_g, norm_g):
    bsz, seq, _ = u_main.shape
    t = GDN_PREP_T
    assert seq % t == 0 and t % HALO == 0
    blk = lambda off: pl.BlockSpec((None, t, GDN_W), lambda b, i: (b, i, off // GDN_W))
    halo = lambda off: pl.BlockSpec(
        (None, HALO, GDN_W), lambda b, i: (b, jnp.maximum(i * (t // HALO) - 1, 0), off // GDN_W))
    full = lambda a: pl.BlockSpec(a.shape, lambda b, i: (0,) * a.ndim)
    out = lambda: pl.BlockSpec((None, t, GDN_W), lambda b, i: (b, i, 0))
    gl_rows = t // CHUNK * SUBLANE
    sds = lambda dt: jax.ShapeDtypeStruct((bsz, seq, GDN_W), dt)
    qg, kd, u, w, aqk, gl = pl.pallas_call(
        _gdn_prep_kernel,
        grid=(bsz, seq // t),
        in_specs=[blk(OFF_DQ), blk(OFF_DK), blk(OFF_DV), halo(OFF_DQ), halo(OFF_DK), halo(OFF_DV),
                  pl.BlockSpec((None, t, LANE), lambda b, i: (b, i, 0)),
                  full(conv_w), full(a_log_row), full(dt_row), full(e_beta), full(e_g)],
        out_specs=[out(), out(), out(), out(), out(),
                   pl.BlockSpec((None, gl_rows, GDN_W), lambda b, i: (b, i, 0))],
        out_shape=[sds(BF16), sds(BF16), sds(F32), sds(BF16), sds(BF16),
                   jax.ShapeDtypeStruct((bsz, seq // CHUNK * SUBLANE, GDN_W), F32)],
        scratch_shapes=[pltpu.VMEM((t, GDN_W), F32)] * 3 + [pltpu.VMEM((3, t + SUBLANE, GDN_W), F32)],
        compiler_params=_params("parallel", "parallel"),
        name="gdn_prep",
    )(u_main, u_main, u_main, u_main, u_main, u_main, u_small, conv_w, a_log_row, dt_row, e_beta, e_g)

    ts = _pick(seq, (384, 256, 128, 64))
    hb = GDN_HEADS
    wid = hb * GDN_DK
    blk = lambda: pl.BlockSpec((None, ts, wid), lambda b, hg, i: (b, i, hg))
    return pl.pallas_call(
        functools.partial(_gdn_scan_kernel, n_chunks=ts // CHUNK, n_heads=hb),
        grid=(bsz, GDN_HEADS // hb, seq // ts),
        in_specs=[blk(), blk(), blk(), blk(), blk(),
                  pl.BlockSpec((None, ts // CHUNK * SUBLANE, wid), lambda b, hg, i: (b, i, hg)),
                  pl.BlockSpec((None, ts, wid), lambda b, hg, i: (b, i, OFF_DZ // wid + hg)),
                  pl.BlockSpec(norm_g.shape, lambda b, hg, i: (0, 0))],
        out_specs=blk(),
        out_shape=jax.ShapeDtypeStruct((bsz, seq, GDN_W), BF16),
        scratch_shapes=[pltpu.VMEM((hb, GDN_DK, GDN_DV), F32)],
        compiler_params=_params("parallel", "parallel", "arbitrary"),
        name="gdn_scan",
    )(qg, kd, u, w, aqk, gl, u_main, norm_g)


SB_RB = 128
SB_LOOK = 2
SB_QT = 2 * SB_RB
SB_EXP_ZERO = 104.0


def _sb_kernel(q_ref, k_ref, v_ref, g_ref, ng_ref, o_ref, acc_ref, run_ref, *, n_invalid):
    seq = q_ref.shape[0]
    rb = SB_RB
    scale = SB_D ** -0.5
    jr = lax.broadcasted_iota(jnp.int32, (2 * rb, 2 * rb), 0)
    sc = lax.broadcasted_iota(jnp.int32, (2 * rb, 2 * rb), 1)
    rr = jnp.where((sc >= rb) | ((jr & (rb - 1)) >= sc), 1.0, 0.0).astype(BF16)
    row_i = lax.broadcasted_iota(jnp.int32, (rb, rb), 0)
    col_i = lax.broadcasted_iota(jnp.int32, (rb, rb), 1)
    below_diag = col_i < row_i

    def softplus(z):
        return jnp.maximum(z, 0.0) + jnp.log(1.0 + jnp.exp2(jnp.abs(z) * -LOG2E))

    def weights(z, sp, run):
        hi = sp.astype(BF16)
        lo = (sp - hi.astype(F32)).astype(BF16)
        cs = _dot(jnp.concatenate([hi, lo], axis=1), rr)
        return jnp.exp(z - cs[:, :rb] - run).astype(BF16), cs[:, rb:]

    def mask(z, sp, vis):
        return jnp.where(vis, z, -jnp.inf), jnp.where(vis, sp, 0.0)

    def finish(rows, acc):
        gate = g_ref[rows, :].astype(F32)
        o_ref[rows, :] = _head_norm_gate(acc, ng_ref[...], gate).astype(o_ref.dtype)

    def row_block(zs, key0, check_valid):
        run = jnp.zeros((rb, rb), F32)
        parts = [None] * len(zs)
        for c in reversed(range(len(zs))):
            z, sp = zs[c], softplus(zs[c])
            vis = below_diag if c == len(zs) - 1 else None
            if check_valid:
                ok = (key0 + c * rb + col_i) >= n_invalid
                vis = ok if vis is None else (vis & ok)
            if vis is not None:
                z, sp = mask(z, sp, vis)
            parts[c], tot = weights(z, sp, run)
            run = run + tot
        return jnp.concatenate(parts, axis=1), run

    def qtile(r0, looks, check_valid):
        k0 = r0 - looks[0] * rb
        nk = looks[0] + 2
        zz = _dot_nt(q_ref[pl.ds(r0, SB_QT), :], k_ref[pl.ds(k0, nk * rb), :]) * scale
        run_min = None
        for r in range(2):
            c0 = r - looks[r] + looks[0]
            zs = [zz[r * rb:(r + 1) * rb, c * rb:(c + 1) * rb] for c in range(c0, c0 + looks[r] + 1)]
            a, run = row_block(zs, k0 + c0 * rb, check_valid)
            acc = _dot(a, v_ref[pl.ds(k0 + c0 * rb, (looks[r] + 1) * rb), :])
            rows = pl.ds(r0 + r * rb, rb)
            acc_ref[rows, :] = acc
            run_ref[rows, :] = run
            finish(rows, acc)
            run_min = run if run_min is None else jnp.minimum(run_min, run)
        return run_min

    first_open = n_invalid // rb + SB_LOOK + 1
    assert first_open % 2 == 0 and first_open * rb <= seq and (first_open - SB_LOOK) * rb >= n_invalid
    qtile(0, (0, 1), True)
    for qi in range(1, first_open // 2):
        qtile(qi * SB_QT, (SB_LOOK, SB_LOOK), True)

    def first_pass(qi, run_min):
        r0 = pl.multiple_of(qi * SB_QT, SB_QT)
        return jnp.minimum(run_min, qtile(r0, (SB_LOOK, SB_LOOK), False))

    run_min = lax.fori_loop(first_open // 2, seq // SB_QT, first_pass,
                            jnp.full((rb, rb), 2 * SB_EXP_ZERO, F32), unroll=4)

    @pl.when(jnp.min(run_min) < SB_EXP_ZERO)
    def _():
        def second_pass(b, carry):
            rows = pl.ds(pl.multiple_of(b * rb, rb), rb)

            def unfinished(state):
                j, lowest = state
                return (j >= 0) & (lowest < SB_EXP_ZERO)

            def sub_block(state):
                j, _ = state
                keys = pl.ds(pl.multiple_of(j * rb, rb), rb)
                z = _dot_nt(q_ref[rows, :], k_ref[keys, :]) * scale
                z, sp = mask(z, softplus(z), (j * rb + col_i) >= n_invalid)
                run = run_ref[rows, :]
                a, tot = weights(z, sp, run)
                acc_ref[rows, :] += _dot(a, v_ref[keys, :])
                run_ref[rows, :] = run + tot
                return j - 1, jnp.min(run + tot)

            lax.while_loop(unfinished, sub_block, (b - SB_LOOK - 1, jnp.min(run_ref[rows, :])))
            finish(rows, acc_ref[rows, :])
            return carry

        lax.fori_loop(first_open, seq // rb, second_pass, 0)


def _sb(u_main, norm_g, n_invalid):
    bsz, seq, _ = u_main.shape
    assert seq % SB_QT == 0 and SB_D == SB_RB
    blk = lambda off: pl.BlockSpec((None, seq, SB_D), lambda b, h: (b, 0, off // SB_D + h))
    return pl.pallas_call(
        functools.partial(_sb_kernel, n_invalid=n_invalid),
        grid=(bsz, SB_HEADS),
        in_specs=[blk(OFF_SQ), blk(OFF_SK), blk(OFF_SV), blk(OFF_SG),
                  pl.BlockSpec(norm_g.shape, lambda b, h: (0, 0))],
        out_specs=pl.BlockSpec((None, seq, SB_D), lambda b, h: (b, 0, h)),
        out_shape=jax.ShapeDtypeStruct((bsz, seq, SB_W), BF16),
        scratch_shapes=[pltpu.VMEM((seq, SB_D), F32), pltpu.VMEM((seq, SB_D), F32)],
        compiler_params=_params("parallel", "parallel"),
        name="stick_breaking",
    )(u_main, u_main, u_main, u_main, norm_g)


def _place(vec, lane0):
    return jnp.zeros((1, LANE), F32).at[0, lane0:lane0 + vec.shape[0]].set(vec.astype(F32))


def _expand(lane0, heads, width):
    src = jnp.arange(LANE)[:, None]
    dst_head = jnp.arange(heads * width)[None, :] // width
    return (src == lane0 + dst_head).astype(BF16)


def _layer(h, xn, w_t, w_out, layer, gla_w_gate, gla_b_gate, gla_norm_g, gdn_conv_w, gdn_a_log,
           gdn_dt_bias, gdn_norm_g, sb_norm_g, bsz, seq):
    m = bsz * seq
    o = _IN_OFF
    base = layer * o[-1]
    w_small = jnp.concatenate(
        [w_t[base + o[4]:base + o[5]], w_t[base + o[9]:base + o[11]],
         jnp.zeros((LANE - GLA_RANK - 2 * GDN_HEADS, w_t.shape[1]), w_t.dtype)], axis=0)

    def project(lo, hi, name):
        return _in_proj(xn, w_t, base + lo, hi - lo, BF16, name).reshape(bsz, seq, hi - lo)

    u_gla = project(o[0], o[4], "in_proj_gla")
    u_gdn = project(o[5], o[9], "in_proj_gdn")
    u_sb = project(o[11], o[15], "in_proj_sb")
    u_small = _in_proj(xn, w_small, 0, LANE, F32, "in_proj_small").reshape(bsz, seq, LANE)

    w_gate_pad = jnp.zeros((LANE, GLA_QK), F32).at[S_LR:S_LR + GLA_RANK].set(gla_w_gate)
    o_gla = _gla(u_gla, u_small, w_gate_pad, gla_b_gate.reshape(1, GLA_QK),
                 gla_norm_g.reshape(1, GLA_DV))
    o_gdn = _gdn(u_gdn, u_small, gdn_conv_w, _place(gdn_a_log, S_A), _place(gdn_dt_bias, S_A),
                 _expand(S_B, GDN_HEADS, GDN_DK), _expand(S_A, GDN_HEADS, GDN_DK),
                 gdn_norm_g.reshape(1, GDN_DV))
    o_sb = _sb(u_sb, sb_norm_g.reshape(1, SB_D), PREFIX - N_META)

    return _out_proj(o_gla.reshape(m, GLA_W), o_gdn.reshape(m, GDN_W), o_sb.reshape(m, SB_W),
                     w_out, layer, h)


def kernel(x, meta, norm_g, w_in, gla_w_gate, gla_b_gate, gla_norm_g, gdn_conv_w, gdn_a_log,
           gdn_dt_bias, gdn_norm_g, sb_norm_g, w_out, final_g):
    bsz, n, d = x.shape
    prefix = jnp.concatenate([jnp.zeros((PREFIX - N_META, d), x.dtype), meta.astype(x.dtype)], axis=0)
    seq = PREFIX + n
    h, xn = _embed_norm(x, prefix, norm_g[0])
    h = h.reshape(bsz * seq, d)
    xn = xn.reshape(bsz * seq, d)
    w_t = jnp.swapaxes(w_in, 1, 2).reshape(-1, d)
    for l in range(norm_g.shape[0]):
        if l > 0:
            xn = _rmsnorm(h, norm_g[l], BF16)
        h = _layer(h, xn, w_t, w_out, l, gla_w_gate[l], gla_b_gate[l], gla_norm_g[l], gdn_conv_w[l],
                   gdn_a_log[l], gdn_dt_bias[l], gdn_norm_g[l], sb_norm_g[l], bsz, seq)
    return _final_rmsnorm(h.reshape(bsz, seq, d), final_g, PREFIX)
```

```python
import functools

import jax
import jax.numpy as jnp
from jax import lax
from jax.experimental import pallas as pl
from jax.experimental.pallas import tpu as pltpu

F32 = jnp.float32
BF16 = jnp.bfloat16
HIGHEST = lax.Precision.HIGHEST

N_META = 16
PREFIX = 256
CHUNK = 64
EPS = 1e-6
LOG2E = 1.4426950408889634

GLA_HEADS, GLA_DK, GLA_DV, GLA_RANK, GLA_TAU = 4, 128, 256, 16, 16.0
GDN_HEADS, GDN_DK, GDN_DV, CONV_K = 12, 128, 128, 4
SB_HEADS, SB_D = 12, 128

GLA_QK = GLA_HEADS * GLA_DK
GLA_W = GLA_HEADS * GLA_DV
GDN_QK = GDN_HEADS * GDN_DK
GDN_W = GDN_HEADS * GDN_DV
SB_W = SB_HEADS * SB_D

_IN_SPLITS = (GLA_QK, GLA_QK, GLA_W, GLA_W, GLA_RANK,
              GDN_QK, GDN_QK, GDN_W, GDN_W, GDN_HEADS, GDN_HEADS,
              SB_W, SB_W, SB_W, SB_W)
_IN_OFF = [0]
for _w in _IN_SPLITS:
    _IN_OFF.append(_IN_OFF[-1] + _w)

OFF_GQ, OFF_GK, OFF_GV, OFF_GR = 0, 512, 1024, 2048
OFF_DQ, OFF_DK, OFF_DV, OFF_DZ = 0, 1536, 3072, 4608
OFF_SQ, OFF_SK, OFF_SV, OFF_SG = 0, 1536, 3072, 4608
LANE = 128
SUBLANE = 8
S_LR, S_B, S_A = 0, GLA_RANK, GLA_RANK + GDN_HEADS

V7X_VMEM_LIMIT_BYTES = 56 * 1024 * 1024


def _pick(n, candidates):
    for c in candidates:
        if n % c == 0:
            return c
    raise ValueError(f"no block size in {candidates} divides {n}")


def _params(*sem):
    return pltpu.CompilerParams(dimension_semantics=sem, vmem_limit_bytes=V7X_VMEM_LIMIT_BYTES)


def _dot(a, b):
    return jnp.dot(a, b, preferred_element_type=F32)


def _dot_nt(a, b):
    return lax.dot_general(a, b, (((1,), (1,)), ((), ())), preferred_element_type=F32)


def _dot_tn(a, b):
    return lax.dot_general(a, b, (((0,), (0,)), ((), ())), preferred_element_type=F32)


def _split_bf16(x, n):
    parts = []
    for _ in range(n):
        p = x.astype(BF16)
        parts.append(p)
        x = x - p.astype(F32)
    return parts


def _dot_f32_by_exact(a, b_exact):
    hi, mid, lo = _split_bf16(a, 3)
    return _dot(hi, b_exact) + _dot(mid, b_exact) + _dot(lo, b_exact)


def _dot_exact_by_f32(a_exact, b):
    hi, mid, lo = _split_bf16(b, 3)
    return _dot(a_exact, hi) + _dot(a_exact, mid) + _dot(a_exact, lo)


def _dot_split2(a_parts, b_parts):
    return _dot(a_parts[0], b_parts[0]) + _dot(a_parts[0], b_parts[1]) + _dot(a_parts[1], b_parts[0])


def _softplus(x):
    return jnp.maximum(x, 0.0) + jnp.log(1.0 + jnp.exp(-jnp.abs(x)))


def _log_sigmoid(x):
    return jnp.minimum(x, 0.0) - jnp.log(1.0 + jnp.exp(-jnp.abs(x)))


def _sigmoid(x):
    return 1.0 / (1.0 + jnp.exp(-x))


def _head_norm_gate(o, norm_g, gate):
    ms = jnp.mean(o * o, axis=-1, keepdims=True)
    return o * lax.rsqrt(ms + EPS) * norm_g * (gate * _sigmoid(gate))


def _rmsnorm_kernel(x_ref, g_ref, o_ref):
    x = x_ref[...]
    ms = jnp.mean(x * x, axis=-1, keepdims=True)
    o_ref[...] = (x * lax.rsqrt(ms + EPS) * g_ref[...]).astype(o_ref.dtype)


def _rmsnorm(x, g, out_dtype):
    m, d = x.shape
    tm = _pick(m, (512, 256, 128, 64, 8))
    return pl.pallas_call(
        _rmsnorm_kernel,
        grid=(m // tm,),
        in_specs=[pl.BlockSpec((tm, d), lambda i: (i, 0)),
                  pl.BlockSpec((1, d), lambda i: (0, 0))],
        out_specs=pl.BlockSpec((tm, d), lambda i: (i, 0)),
        out_shape=jax.ShapeDtypeStruct((m, d), out_dtype),
        compiler_params=_params("parallel"),
        name="rmsnorm",
    )(x, g.reshape(1, d))


def _embed_norm_kernel(x_ref, p_ref, g_ref, h_ref, o_ref):
    x = jnp.where(pl.program_id(1) == 0, p_ref[...], x_ref[...])
    h_ref[...] = x
    ms = jnp.mean(x * x, axis=-1, keepdims=True)
    o_ref[...] = (x * lax.rsqrt(ms + EPS) * g_ref[...]).astype(o_ref.dtype)


def _embed_norm(x, prefix, g):
    bsz, n, d = x.shape
    tm = prefix.shape[0]
    assert n % tm == 0
    blk = lambda: pl.BlockSpec((None, tm, d), lambda b, i: (b, i, 0))
    return pl.pallas_call(
        _embed_norm_kernel,
        grid=(bsz, n // tm + 1),
        in_specs=[pl.BlockSpec((None, tm, d), lambda b, i: (b, jnp.maximum(i - 1, 0), 0)),
                  pl.BlockSpec((tm, d), lambda b, i: (0, 0)),
                  pl.BlockSpec((1, d), lambda b, i: (0, 0))],
        out_specs=[blk(), blk()],
        out_shape=[jax.ShapeDtypeStruct((bsz, n + tm, d), x.dtype),
                   jax.ShapeDtypeStruct((bsz, n + tm, d), BF16)],
        compiler_params=_params("parallel", "parallel"),
        name="embed_norm",
    )(x, prefix, g.reshape(1, d))


def _final_rmsnorm(h, g, n_skip):
    bsz, seq, d = h.shape
    tm = _pick(n_skip, (256, 128, 64, 8))
    assert (seq - n_skip) % tm == 0
    return pl.pallas_call(
        _rmsnorm_kernel,
        grid=(bsz, (seq - n_skip) // tm),
        in_specs=[pl.BlockSpec((None, tm, d), lambda b, i: (b, i + n_skip // tm, 0)),
                  pl.BlockSpec((1, d), lambda b, i: (0, 0))],
        out_specs=pl.BlockSpec((None, tm, d), lambda b, i: (b, i, 0)),
        out_shape=jax.ShapeDtypeStruct((bsz, seq - n_skip, d), h.dtype),
        compiler_params=_params("parallel", "parallel"),
        name="final_rmsnorm",
    )(h, g.reshape(1, d))


def _matmul_kernel(x_ref, w_ref, o_ref):
    o_ref[...] = _dot(x_ref[...], w_ref[...]).astype(o_ref.dtype)


def _matmul(x, w, out_dtype, name):
    m, k = x.shape
    n = w.shape[1]
    tm = _pick(m, (1536, 1408, 1024, 768, 512, 256, 128))
    tn = _pick(n, (512, 256, 128))
    return pl.pallas_call(
        _matmul_kernel,
        grid=(m // tm, n // tn),
        in_specs=[pl.BlockSpec((tm, k), lambda i, j: (i, 0)),
                  pl.BlockSpec((k, tn), lambda i, j: (0, j))],
        out_specs=pl.BlockSpec((tm, tn), lambda i, j: (i, j)),
        out_shape=jax.ShapeDtypeStruct((m, n), out_dtype),
        compiler_params=_params("parallel", "parallel"),
        name=name,
    )(x, w)


PROJ_TN = 512
CAST_ROWS = 512


def _cast_weight_tile(dst_ref, src_ref):
    step = min(CAST_ROWS, dst_ref.shape[0])

    def chunk(r, carry):
        rows = pl.ds(pl.multiple_of(r * step, step), step)
        dst_ref[rows, :] = src_ref[rows, :].astype(BF16)
        return carry

    lax.fori_loop(0, dst_ref.shape[0] // step, chunk, 0)


def _in_proj_kernel(x_ref, wt_ref, o_ref, w_ref):
    @pl.when(pl.program_id(1) == 0)
    def _():
        for c in range(wt_ref.shape[0] // LANE):
            cols = slice(c * LANE, (c + 1) * LANE)
            w_ref[:, cols] = wt_ref[cols, :].T.astype(BF16)

    o_ref[...] = _dot(x_ref[...], w_ref[...]).astype(o_ref.dtype)


def _in_proj(xn, w_t, row0, width, out_dtype, name):
    m, k = xn.shape
    tn = min(PROJ_TN, width)
    assert width % tn == 0 and row0 % SUBLANE == 0 and tn % LANE == 0
    tm = _pick(m, (1536, 1056, 768, 512, 256, 128))
    return pl.pallas_call(
        _in_proj_kernel,
        grid=(width // tn, m // tm),
        in_specs=[pl.BlockSpec((tm, k), lambda j, i: (i, 0)),
                  pl.BlockSpec((pl.Element(tn), pl.Element(k)),
                               lambda j, i: (pl.multiple_of(row0 + j * tn, SUBLANE), 0))],
        out_specs=pl.BlockSpec((tm, tn), lambda j, i: (i, j)),
        out_shape=jax.ShapeDtypeStruct((m, width), out_dtype),
        scratch_shapes=[pltpu.VMEM((k, tn), BF16)],
        compiler_params=_params("arbitrary", "arbitrary"),
        name=name,
    )(xn, w_t)


def _out_proj_kernel(a1_ref, a2_ref, a3_ref, wf_ref, h_ref, o_ref, w_ref):
    @pl.when(pl.program_id(1) == 0)
    def _():
        _cast_weight_tile(w_ref, wf_ref)

    k1, k2 = a1_ref.shape[1], a2_ref.shape[1]
    y = _dot(a1_ref[...], w_ref[0:k1, :])
    y = y + _dot(a2_ref[...], w_ref[k1:k1 + k2, :])
    y = y + _dot(a3_ref[...], w_ref[k1 + k2:, :])
    o_ref[...] = h_ref[...] + y


def _out_proj(a1, a2, a3, w_out, layer, h):
    m, d = h.shape
    k = w_out.shape[1]
    tn = PROJ_TN
    assert a1.shape[1] + a2.shape[1] + a3.shape[1] == k and d % tn == 0 and k % CAST_ROWS == 0
    tm = _pick(m, (1056, 768, 512, 256, 128))
    act = lambda a: pl.BlockSpec((tm, a.shape[1]), lambda j, i: (i, 0))
    return pl.pallas_call(
        _out_proj_kernel,
        grid=(d // tn, m // tm),
        in_specs=[act(a1), act(a2), act(a3),
                  pl.BlockSpec((None, k, tn), lambda j, i: (layer, 0, j)),
                  pl.BlockSpec((tm, tn), lambda j, i: (i, j))],
        out_specs=pl.BlockSpec((tm, tn), lambda j, i: (i, j)),
        out_shape=jax.ShapeDtypeStruct((m, d), F32),
        scratch_shapes=[pltpu.VMEM((k, tn), BF16)],
        compiler_params=_params("arbitrary", "arbitrary"),
        name="out_proj",
    )(a1, a2, a3, w_out, h)


def _gla_kernel(q_ref, k_ref, v_ref, r_ref, s_ref, wg_ref, bg_ref, ng_ref, o_ref, st_ref,
                *, n_chunks):
    @pl.when(pl.program_id(1) == 0)
    def _():
        st_ref[...] = jnp.zeros_like(st_ref)

    row = lax.broadcasted_iota(jnp.int32, (CHUNK, CHUNK), 0)
    col = lax.broadcasted_iota(jnp.int32, (CHUNK, CHUNK), 1)
    causal = row >= col
    tri = jnp.where(causal, 1.0, 0.0).astype(BF16)
    scale = GLA_DK ** -0.5
    wg_parts = _split_bf16(wg_ref[...], 2)

    def chunk(c, carry):
        rows = pl.ds(pl.multiple_of(c * CHUNK, CHUNK), CHUNK)
        x = _dot_split2(_split_bf16(s_ref[rows, :], 2), wg_parts) + bg_ref[...]
        g = _log_sigmoid(x) * (1.0 / GLA_TAU)
        b = _dot_exact_by_f32(tri, g)
        b_last = b[CHUNK - 1:CHUNK, :]
        q = q_ref[rows, :].astype(F32)
        k = k_ref[rows, :].astype(F32)
        qt = (q * scale * jnp.exp(b)).astype(BF16)
        kt = (k * jnp.exp(-b)).astype(BF16)
        kl = (k * jnp.exp(b_last - b)).astype(BF16)
        dec = jnp.exp(b_last)
        heads = range(GLA_HEADS)
        ks = [slice(h * GLA_DK, (h + 1) * GLA_DK) for h in heads]
        vs = [slice(h * GLA_DV, (h + 1) * GLA_DV) for h in heads]
        att = [jnp.where(causal, _dot_nt(qt[:, ks[h]], kt[:, ks[h]]), 0.0).astype(BF16) for h in heads]
        st = [st_ref[h] for h in heads]
        vh = [v_ref[rows, vs[h]] for h in heads]
        o = [_dot(att[h], vh[h]) + _dot_nt(qt[:, ks[h]], st[h].astype(BF16)) for h in heads]
        for h in heads:
            st_ref[h] = st[h] * dec[:, ks[h]] + _dot_tn(vh[h], kl[:, ks[h]])
        for h in heads:
            gate = r_ref[rows, vs[h]].astype(F32)
            o_ref[rows, vs[h]] = _head_norm_gate(o[h], ng_ref[...], gate).astype(o_ref.dtype)
        return carry

    lax.fori_loop(0, n_chunks, chunk, 0, unroll=4)


def _gla(u_main, u_small, w_gate_pad, b_gate, norm_g):
    bsz, seq, _ = u_main.shape
    t = _pick(seq, (768, 384, 256, 128, 64))
    col = lambda off, w: pl.BlockSpec((None, t, w), lambda b, i: (b, i, off // w))
    full = lambda a: pl.BlockSpec(a.shape, lambda b, i: (0,) * a.ndim)
    return pl.pallas_call(
        functools.partial(_gla_kernel, n_chunks=t // CHUNK),
        grid=(bsz, seq // t),
        in_specs=[col(OFF_GQ, GLA_QK), col(OFF_GK, GLA_QK), col(OFF_GV, GLA_W), col(OFF_GR, GLA_W),
                  pl.BlockSpec((None, t, LANE), lambda b, i: (b, i, 0)),
                  full(w_gate_pad), full(b_gate), full(norm_g)],
        out_specs=pl.BlockSpec((None, t, GLA_W), lambda b, i: (b, i, 0)),
        out_shape=jax.ShapeDtypeStruct((bsz, seq, GLA_W), BF16),
        scratch_shapes=[pltpu.VMEM((GLA_HEADS, GLA_DV, GLA_DK), F32)],
        compiler_params=_params("parallel", "arbitrary"),
        name="gla",
    )(u_main, u_main, u_main, u_main, u_small, w_gate_pad, b_gate, norm_g)


GDN_PREP_T = 128
HALO = 16


def _gdn_prep_kernel(q_ref, k_ref, v_ref, qh_ref, kh_ref, vh_ref, s_ref, cw_ref, al_ref, dt_ref,
                     eb_ref, eg_ref, qg_ref, kd_ref, u_ref, w_ref, aqk_ref, gl_ref,
                     qs_ref, ks_ref, vs_ref, xe_ref):
    first = pl.program_id(1) == 0
    t = q_ref.shape[0]

    def conv_silu(x_ref, h_ref, xe_ref, w):
        xe_ref[0:SUBLANE, :] = jnp.where(first, 0.0, h_ref[...].astype(F32)[HALO - SUBLANE:, :])
        xe_ref[SUBLANE:, :] = x_ref[...].astype(F32)
        y = xe_ref[SUBLANE:, :] * w[CONV_K - 1:CONV_K, :]
        for j in range(1, CONV_K):
            y = y + xe_ref[SUBLANE - j:SUBLANE - j + t, :] * w[CONV_K - 1 - j:CONV_K - j, :]
        return y * _sigmoid(y)

    qc = conv_silu(q_ref, qh_ref, xe_ref.at[0], cw_ref[:, 0:GDN_QK])
    kc = conv_silu(k_ref, kh_ref, xe_ref.at[1], cw_ref[:, GDN_QK:2 * GDN_QK])
    vs_ref[...] = conv_silu(v_ref, vh_ref, xe_ref.at[2], cw_ref[:, 2 * GDN_QK:2 * GDN_QK + GDN_W])
    for h in range(GDN_HEADS):
        sl = slice(h * GDN_DK, (h + 1) * GDN_DK)
        qh = qc[:, sl]
        kh = kc[:, sl]
        qs_ref[:, sl] = qh * (lax.rsqrt(jnp.sum(qh * qh, axis=-1, keepdims=True) + EPS)
                              * GDN_DK ** -0.5)
        ks_ref[:, sl] = kh * lax.rsqrt(jnp.sum(kh * kh, axis=-1, keepdims=True) + EPS)

    s = s_ref[...]
    beta = _sigmoid(s)
    g = -jnp.exp(al_ref[...]) * _softplus(s + dt_ref[...])

    row = lax.broadcasted_iota(jnp.int32, (CHUNK, CHUNK), 0)
    col = lax.broadcasted_iota(jnp.int32, (CHUNK, CHUNK), 1)
    causal = row >= col
    strict = row > col
    tri = jnp.where(causal, 1.0, 0.0).astype(BF16)
    zeros_half = jnp.zeros((CHUNK, GDN_DV - CHUNK), F32)

    pairs = [(c, h) for c in range(t // CHUNK) for h in range(GDN_HEADS)]
    a_neg, rhs = {}, {}
    for c in range(t // CHUNK):
        rows = slice(c * CHUNK, (c + 1) * CHUNK)
        gcum = _dot_exact_by_f32(tri, g[rows])
        gcum_t = gcum.T
        gb = _dot_f32_by_exact(gcum, eg_ref[...])
        bb = _dot_f32_by_exact(beta[rows], eb_ref[...])
        g_last = gb[CHUNK - 1:CHUNK, :]
        eg = jnp.exp(gb)
        q = qs_ref[rows, :]
        k = ks_ref[rows, :]
        kb = k * bb
        qg_ref[rows, :] = (q * eg).astype(BF16)
        kd_ref[rows, :] = (k * jnp.exp(g_last - gb)).astype(BF16)
        gl_ref[c * SUBLANE:(c + 1) * SUBLANE, :] = jnp.broadcast_to(jnp.exp(g_last), (SUBLANE, GDN_W))
        rhs_v = vs_ref[rows, :] * bb
        rhs_k = kb * eg
        for h in range(GDN_HEADS):
            sl = slice(h * GDN_DK, (h + 1) * GDN_DK)
            lhs = jnp.concatenate([kb[:, sl], q[:, sl]], axis=0).astype(BF16)
            pr = _dot_nt(lhs, k[:, sl].astype(BF16))
            g_t = gb[:, h * GDN_DK:h * GDN_DK + CHUNK]
            g_s = gcum_t[S_A + h:S_A + h + 1, :]
            dec = jnp.exp(jnp.where(causal, g_t - g_s, -jnp.inf))
            a_neg[c, h] = jnp.where(strict, -pr[:CHUNK] * dec, 0.0)
            aqk_ref[rows, sl] = jnp.concatenate([pr[CHUNK:] * dec, zeros_half], axis=1).astype(BF16)
            rhs[c, h] = jnp.concatenate([rhs_v[:, sl], rhs_k[:, sl]], axis=1)
    toff = dict(a_neg)
    p = dict(a_neg)
    for _ in range(5):
        for key in pairs:
            pb = p[key].astype(BF16)
            p[key] = _dot(pb, pb)
        for key in pairs:
            toff[key] = toff[key] + p[key] + _dot(toff[key].astype(BF16), p[key].astype(BF16))
    for c, h in pairs:
        rows = slice(c * CHUNK, (c + 1) * CHUNK)
        sl = slice(h * GDN_DK, (h + 1) * GDN_DK)
        sol = rhs[c, h] + _dot(toff[c, h].astype(BF16), rhs[c, h].astype(BF16))
        u_ref[rows, sl] = sol[:, :GDN_DV]
        w_ref[rows, sl] = sol[:, GDN_DV:].astype(BF16)


def _gdn_scan_kernel(qg_ref, kd_ref, u_ref, w_ref, aqk_ref, gl_ref, z_ref, ng_ref, o_ref, st_ref,
                     *, n_chunks, n_heads):
    @pl.when(pl.program_id(2) == 0)
    def _():
        st_ref[...] = jnp.zeros_like(st_ref)

    def chunk(c, carry):
        rows = pl.ds(pl.multiple_of(c * CHUNK, CHUNK), CHUNK)
        grow = pl.ds(pl.multiple_of(c * SUBLANE, SUBLANE), SUBLANE)
        heads = range(n_heads)
        sls = [slice(h * GDN_DK, (h + 1) * GDN_DK) for h in heads]
        st = [st_ref[h] for h in heads]
        ws = [_dot(jnp.concatenate([w_ref[rows, sls[h]], qg_ref[rows, sls[h]]], axis=0),
                   st[h].astype(BF16)) for h in heads]
        vn = [(u_ref[rows, sls[h]] - ws[h][:CHUNK]).astype(BF16) for h in heads]
        o = [ws[h][CHUNK:] + _dot(aqk_ref[rows, h * GDN_DK:h * GDN_DK + CHUNK], vn[h]) for h in heads]
        for h in heads:
            st_ref[h] = st[h] * gl_ref[grow, sls[h]][0:1, :] + _dot_tn(kd_ref[rows, sls[h]], vn[h])
        for h in heads:
            gate = z_ref[rows, sls[h]].astype(F32)
            o_ref[rows, sls[h]] = _head_norm_gate(o[h], ng_ref[...], gate).astype(o_ref.dtype)
        return carry

    lax.fori_loop(0, n_chunks, chunk, 0, unroll=2)


def _gdn(u_main, u_small, conv_w, a_log_row, dt_row, e_beta, e_g, norm_g):
    bsz, seq, _ = u_main.shape
    t = GDN_PREP_T
    assert seq % t == 0 and t % HALO == 0
    blk = lambda off: pl.BlockSpec((None, t, GDN_W), lambda b, i: (b, i, off // GDN_W))
    halo = lambda off: pl.BlockSpec(
        (None, HALO, GDN_W), lambda b, i: (b, jnp.maximum(i * (t // HALO) - 1, 0), off // GDN_W))
    full = lambda a: pl.BlockSpec(a.shape, lambda b, i: (0,) * a.ndim)
    out = lambda: pl.BlockSpec((None, t, GDN_W), lambda b, i: (b, i, 0))
    gl_rows = t // CHUNK * SUBLANE
    sds = lambda dt: jax.ShapeDtypeStruct((bsz, seq, GDN_W), dt)
    qg, kd, u, w, aqk, gl = pl.pallas_call(
        _gdn_prep_kernel,
        grid=(bsz, seq // t),
        in_specs=[blk(OFF_DQ), blk(OFF_DK), blk(OFF_DV), halo(OFF_DQ), halo(OFF_DK), halo(OFF_DV),
                  pl.BlockSpec((None, t, LANE), lambda b, i: (b, i, 0)),
                  full(conv_w), full(a_log_row), full(dt_row), full(e_beta), full(e_g)],
        out_specs=[out(), out(), out(), out(), out(),
                   pl.BlockSpec((None, gl_rows, GDN_W), lambda b, i: (b, i, 0))],
        out_shape=[sds(BF16), sds(BF16), sds(F32), sds(BF16), sds(BF16),
                   jax.ShapeDtypeStruct((bsz, seq // CHUNK * SUBLANE, GDN_W), F32)],
        scratch_shapes=[pltpu.VMEM((t, GDN_W), F32)] * 3 + [pltpu.VMEM((3, t + SUBLANE, GDN_W), F32)],
        compiler_params=_params("parallel", "parallel"),
        name="gdn_prep",
    )(u_main, u_main, u_main, u_main, u_main, u_main, u_small, conv_w, a_log_row, dt_row, e_beta, e_g)

    ts = _pick(seq, (384, 256, 128, 64))
    hb = GDN_HEADS
    wid = hb * GDN_DK
    blk = lambda: pl.BlockSpec((None, ts, wid), lambda b, hg, i: (b, i, hg))
    return pl.pallas_call(
        functools.partial(_gdn_scan_kernel, n_chunks=ts // CHUNK, n_heads=hb),
        grid=(bsz, GDN_HEADS // hb, seq // ts),
        in_specs=[blk(), blk(), blk(), blk(), blk(),
                  pl.BlockSpec((None, ts // CHUNK * SUBLANE, wid), lambda b, hg, i: (b, i, hg)),
                  pl.BlockSpec((None, ts, wid), lambda b, hg, i: (b, i, OFF_DZ // wid + hg)),
                  pl.BlockSpec(norm_g.shape, lambda b, hg, i: (0, 0))],
        out_specs=blk(),
        out_shape=jax.ShapeDtypeStruct((bsz, seq, GDN_W), BF16),
        scratch_shapes=[pltpu.VMEM((hb, GDN_DK, GDN_DV), F32)],
        compiler_params=_params("parallel", "parallel", "arbitrary"),
        name="gdn_scan",
    )(qg, kd, u, w, aqk, gl, u_main, norm_g)


SB_RB = 128
SB_LOOK = 2
SB_QT = 2 * SB_RB
SB_EXP_ZERO = 104.0


def _sb_kernel(q_ref, k_ref, v_ref, g_ref, ng_ref, o_ref, acc_ref, run_ref, z_buf, hl_buf, a_buf,
               *, n_invalid):
    seq = q_ref.shape[0]
    rb = SB_RB
    scale = SB_D ** -0.5
    jr = lax.broadcasted_iota(jnp.int32, (2 * rb, 2 * rb), 0)
    sc = lax.broadcasted_iota(jnp.int32, (2 * rb, 2 * rb), 1)
    rr = jnp.where((sc >= rb) | ((jr & (rb - 1)) >= sc), 1.0, 0.0).astype(BF16)
    row_i = lax.broadcasted_iota(jnp.int32, (rb, rb), 0)
    col_i = lax.broadcasted_iota(jnp.int32, (rb, rb), 1)
    below_diag = col_i < row_i

    def softplus(z):
        return jnp.maximum(z, 0.0) + jnp.log(1.0 + jnp.exp2(jnp.abs(z) * -LOG2E))

    def weights(z, sp, run):
        hi = sp.astype(BF16)
        lo = (sp - hi.astype(F32)).astype(BF16)
        cs = _dot(jnp.concatenate([hi, lo], axis=1), rr)
        return jnp.exp(z - cs[:, :rb] - run).astype(BF16), cs[:, rb:]

    def mask(z, sp, vis):
        return jnp.where(vis, z, -jnp.inf), jnp.where(vis, sp, 0.0)

    def finish(rows, acc):
        gate = g_ref[rows, :].astype(F32)
        o_ref[rows, :] = _head_norm_gate(acc, ng_ref[...], gate).astype(o_ref.dtype)

    def row_block(zs, key0, check_valid):
        run = jnp.zeros((rb, rb), F32)
        parts = [None] * len(zs)
        for c in reversed(range(len(zs))):
            z, sp = zs[c], softplus(zs[c])
            vis = below_diag if c == len(zs) - 1 else None
            if check_valid:
                ok = (key0 + c * rb + col_i) >= n_invalid
                vis = ok if vis is None else (vis & ok)
            if vis is not None:
                z, sp = mask(z, sp, vis)
            parts[c], tot = weights(z, sp, run)
            run = run + tot
        return jnp.concatenate(parts, axis=1), run

    def qtile(r0, looks, check_valid):
        k0 = r0 - looks[0] * rb
        nk = looks[0] + 2
        zz = _dot_nt(q_ref[pl.ds(r0, SB_QT), :], k_ref[pl.ds(k0, nk * rb), :]) * scale
        run_min = None
        for r in range(2):
            c0 = r - looks[r] + looks[0]
            zs = [zz[r * rb:(r + 1) * rb, c * rb:(c + 1) * rb] for c in range(c0, c0 + looks[r] + 1)]
            a, run = row_block(zs, k0 + c0 * rb, check_valid)
            acc = _dot(a, v_ref[pl.ds(k0 + c0 * rb, (looks[r] + 1) * rb), :])
            rows = pl.ds(r0 + r * rb, rb)
            acc_ref[rows, :] = acc
            run_ref[rows, :] = run
            finish(rows, acc)
            run_min = run if run_min is None else jnp.minimum(run_min, run)
        return run_min

    first_open = n_invalid // rb + SB_LOOK + 1
    assert first_open % 2 == 0 and first_open * rb <= seq and (first_open - SB_LOOK) * rb >= n_invalid
    qtile(0, (0, 1), True)
    for qi in range(1, first_open // 2):
        qtile(qi * SB_QT, (SB_LOOK, SB_LOOK), True)

    n_sub = SB_LOOK + 1
    t0 = first_open // 2
    n_tiles = seq // SB_QT - t0

    def tile_row(t):
        r0 = (t0 + t) * SB_QT
        return r0 if isinstance(t, int) else pl.multiple_of(r0, SB_QT)

    def scores(t):
        r0 = tile_row(t)
        k0 = r0 - SB_LOOK * rb
        zz = _dot_nt(q_ref[pl.ds(r0, SB_QT), :], k_ref[pl.ds(k0, (SB_LOOK + 2) * rb), :]) * scale
        for r in range(2):
            for c in range(n_sub):
                z = zz[r * rb:(r + 1) * rb, (r + c) * rb:(r + c + 1) * rb]
                sp = softplus(z)
                if c == n_sub - 1:
                    z, sp = mask(z, sp, below_diag)
                hi = sp.astype(BF16)
                lo = (sp - hi.astype(F32)).astype(BF16)
                z_buf[r, c] = z
                hl_buf[r, c] = jnp.concatenate([hi, lo], axis=1)

    def weights_of(t):
        r0 = tile_row(t)
        run_min = None
        for r in range(2):
            run = jnp.zeros((rb, rb), F32)
            for c in reversed(range(n_sub)):
                cs = _dot(hl_buf[r, c], rr)
                a_buf[r, :, c * rb:(c + 1) * rb] = jnp.exp(z_buf[r, c] - cs[:, :rb] - run).astype(BF16)
                run = run + cs[:, rb:]
            run_ref[pl.ds(r0 + r * rb, rb), :] = run
            run_min = run if run_min is None else jnp.minimum(run_min, run)
        return run_min

    def values(t):
        r0 = tile_row(t)
        k0 = r0 - SB_LOOK * rb
        for r in range(2):
            acc = _dot(a_buf[r], v_ref[pl.ds(k0 + r * rb, n_sub * rb), :])
            rows = pl.ds(r0 + r * rb, rb)
            acc_ref[rows, :] = acc
            finish(rows, acc)

    def step(s, run_min, stages):
        if stages[2]:
            values(s - 2)
        if stages[1]:
            run_min = jnp.minimum(run_min, weights_of(s - 1))
        if stages[0]:
            scores(s)
        return run_min

    run_min = jnp.full((rb, rb), 2 * SB_EXP_ZERO, F32)
    for s in range(n_tiles + 2):
        if s == 2 and n_tiles > 2:
            run_min = lax.fori_loop(2, n_tiles, lambda t, m: step(t, m, (True, True, True)), run_min,
                                    unroll=2)
        if 2 <= s < n_tiles:
            continue
        run_min = step(s, run_min, [0 <= s - k < n_tiles for k in (0, 1, 2)])

    @pl.when(jnp.min(run_min) < SB_EXP_ZERO)
    def _():
        def second_pass(b, carry):
            rows = pl.ds(pl.multiple_of(b * rb, rb), rb)

            def unfinished(state):
                j, lowest = state
                return (j >= 0) & (lowest < SB_EXP_ZERO)

            def sub_block(state):
                j, _ = state
                keys = pl.ds(pl.multiple_of(j * rb, rb), rb)
                z = _dot_nt(q_ref[rows, :], k_ref[keys, :]) * scale
                z, sp = mask(z, softplus(z), (j * rb + col_i) >= n_invalid)
                run = run_ref[rows, :]
                a, tot = weights(z, sp, run)
                acc_ref[rows, :] += _dot(a, v_ref[keys, :])
                run_ref[rows, :] = run + tot
                return j - 1, jnp.min(run + tot)

            lax.while_loop(unfinished, sub_block, (b - SB_LOOK - 1, jnp.min(run_ref[rows, :])))
            finish(rows, acc_ref[rows, :])
            return carry

        lax.fori_loop(first_open, seq // rb, second_pass, 0)


def _sb(u_main, norm_g, n_invalid):
    bsz, seq, _ = u_main.shape
    assert seq % SB_QT == 0 and SB_D == SB_RB
    blk = lambda off: pl.BlockSpec((None, seq, SB_D), lambda b, h: (b, 0, off // SB_D + h))
    return pl.pallas_call(
        functools.partial(_sb_kernel, n_invalid=n_invalid),
        grid=(bsz, SB_HEADS),
        in_specs=[blk(OFF_SQ), blk(OFF_SK), blk(OFF_SV), blk(OFF_SG),
                  pl.BlockSpec(norm_g.shape, lambda b, h: (0, 0))],
        out_specs=pl.BlockSpec((None, seq, SB_D), lambda b, h: (b, 0, h)),
        out_shape=jax.ShapeDtypeStruct((bsz, seq, SB_W), BF16),
        scratch_shapes=[pltpu.VMEM((seq, SB_D), F32), pltpu.VMEM((seq, SB_D), F32),
                        pltpu.VMEM((2, SB_LOOK + 1, SB_RB, SB_RB), F32),
                        pltpu.VMEM((2, SB_LOOK + 1, SB_RB, 2 * SB_RB), BF16),
                        pltpu.VMEM((2, SB_RB, (SB_LOOK + 1) * SB_RB), BF16)],
        compiler_params=_params("parallel", "parallel"),
        name="stick_breaking",
    )(u_main, u_main, u_main, u_main, norm_g)


def _place(vec, lane0):
    return jnp.zeros((1, LANE), F32).at[0, lane0:lane0 + vec.shape[0]].set(vec.astype(F32))


def _expand(lane0, heads, width):
    src = jnp.arange(LANE)[:, None]
    dst_head = jnp.arange(heads * width)[None, :] // width
    return (src == lane0 + dst_head).astype(BF16)


def _layer(h, xn, w_t, w_out, layer, gla_w_gate, gla_b_gate, gla_norm_g, gdn_conv_w, gdn_a_log,
           gdn_dt_bias, gdn_norm_g, sb_norm_g, bsz, seq):
    m = bsz * seq
    o = _IN_OFF
    base = layer * o[-1]
    w_small = jnp.concatenate(
        [w_t[base + o[4]:base + o[5]], w_t[base + o[9]:base + o[11]],
         jnp.zeros((LANE - GLA_RANK - 2 * GDN_HEADS, w_t.shape[1]), w_t.dtype)], axis=0)

    def project(lo, hi, name):
        return _in_proj(xn, w_t, base + lo, hi - lo, BF16, name).reshape(bsz, seq, hi - lo)

    u_gla = project(o[0], o[4], "in_proj_gla")
    u_gdn = project(o[5], o[9], "in_proj_gdn")
    u_sb = project(o[11], o[15], "in_proj_sb")
    u_small = _in_proj(xn, w_small, 0, LANE, F32, "in_proj_small").reshape(bsz, seq, LANE)

    w_gate_pad = jnp.zeros((LANE, GLA_QK), F32).at[S_LR:S_LR + GLA_RANK].set(gla_w_gate)
    o_gla = _gla(u_gla, u_small, w_gate_pad, gla_b_gate.reshape(1, GLA_QK),
                 gla_norm_g.reshape(1, GLA_DV))
    o_gdn = _gdn(u_gdn, u_small, gdn_conv_w, _place(gdn_a_log, S_A), _place(gdn_dt_bias, S_A),
                 _expand(S_B, GDN_HEADS, GDN_DK), _expand(S_A, GDN_HEADS, GDN_DK),
                 gdn_norm_g.reshape(1, GDN_DV))
    o_sb = _sb(u_sb, sb_norm_g.reshape(1, SB_D), PREFIX - N_META)

    return _out_proj(o_gla.reshape(m, GLA_W), o_gdn.reshape(m, GDN_W), o_sb.reshape(m, SB_W),
                     w_out, layer, h)


def kernel(x, meta, norm_g, w_in, gla_w_gate, gla_b_gate, gla_norm_g, gdn_conv_w, gdn_a_log,
           gdn_dt_bias, gdn_norm_g, sb_norm_g, w_out, final_g):
    bsz, n, d = x.shape
    prefix = jnp.concatenate([jnp.zeros((PREFIX - N_META, d), x.dtype), meta.astype(x.dtype)], axis=0)
    seq = PREFIX + n
    h, xn = _embed_norm(x, prefix, norm_g[0])
    h = h.reshape(bsz * seq, d)
    xn = xn.reshape(bsz * seq, d)
    w_t = jnp.swapaxes(w_in, 1, 2).reshape(-1, d)
    for l in range(norm_g.shape[0]):
        if l > 0:
            xn = _rmsnorm(h, norm_g[l], BF16)
        h = _layer(h, xn, w_t, w_out, l, gla_w_gate[l], gla_b_gate[l], gla_norm_g[l], gdn_conv_w[l],
                   gdn_a_log[l], gdn_dt_bias[l], gdn_norm_g[l], sb_norm_g[l], bsz, seq)
    return _final_rmsnorm(h.reshape(bsz, seq, d), final_g, PREFIX)
```

```python
import functools

import jax
import jax.numpy as jnp
from jax import lax
from jax.experimental import pallas as pl
from jax.experimental.pallas import tpu as pltpu

F32 = jnp.float32
BF16 = jnp.bfloat16
HIGHEST = lax.Precision.HIGHEST

N_META = 16
PREFIX = 256
CHUNK = 64
EPS = 1e-6
LOG2E = 1.4426950408889634

GLA_HEADS, GLA_DK, GLA_DV, GLA_RANK, GLA_TAU = 4, 128, 256, 16, 16.0
GDN_HEADS, GDN_DK, GDN_DV, CONV_K = 12, 128, 128, 4
SB_HEADS, SB_D = 12, 128

GLA_QK = GLA_HEADS * GLA_DK
GLA_W = GLA_HEADS * GLA_DV
GDN_QK = GDN_HEADS * GDN_DK
GDN_W = GDN_HEADS * GDN_DV
SB_W = SB_HEADS * SB_D

_IN_SPLITS = (GLA_QK, GLA_QK, GLA_W, GLA_W, GLA_RANK,
              GDN_QK, GDN_QK, GDN_W, GDN_W, GDN_HEADS, GDN_HEADS,
              SB_W, SB_W, SB_W, SB_W)
_IN_OFF = [0]
for _w in _IN_SPLITS:
    _IN_OFF.append(_IN_OFF[-1] + _w)

OFF_GQ, OFF_GK, OFF_GV, OFF_GR = 0, 512, 1024, 2048
OFF_DQ, OFF_DK, OFF_DV, OFF_DZ = 0, 1536, 3072, 4608
OFF_SQ, OFF_SK, OFF_SV, OFF_SG = 0, 1536, 3072, 4608
LANE = 128
SUBLANE = 8
S_LR, S_B, S_A = 0, GLA_RANK, GLA_RANK + GDN_HEADS

V7X_VMEM_LIMIT_BYTES = 56 * 1024 * 1024


def _pick(n, candidates):
    for c in candidates:
        if n % c == 0:
            return c
    raise ValueError(f"no block size in {candidates} divides {n}")


def _params(*sem):
    return pltpu.CompilerParams(dimension_semantics=sem, vmem_limit_bytes=V7X_VMEM_LIMIT_BYTES)


def _dot(a, b):
    return jnp.dot(a, b, preferred_element_type=F32)


def _dot_nt(a, b):
    return lax.dot_general(a, b, (((1,), (1,)), ((), ())), preferred_element_type=F32)


def _dot_tn(a, b):
    return lax.dot_general(a, b, (((0,), (0,)), ((), ())), preferred_element_type=F32)


def _split_bf16(x, n):
    parts = []
    for _ in range(n):
        p = x.astype(BF16)
        parts.append(p)
        x = x - p.astype(F32)
    return parts


def _dot_f32_by_exact(a, b_exact):
    hi, mid, lo = _split_bf16(a, 3)
    return _dot(hi, b_exact) + _dot(mid, b_exact) + _dot(lo, b_exact)


def _dot_exact_by_f32(a_exact, b):
    hi, mid, lo = _split_bf16(b, 3)
    return _dot(a_exact, hi) + _dot(a_exact, mid) + _dot(a_exact, lo)


def _dot_split2(a_parts, b_parts):
    return _dot(a_parts[0], b_parts[0]) + _dot(a_parts[0], b_parts[1]) + _dot(a_parts[1], b_parts[0])


def _softplus(x):
    return jnp.maximum(x, 0.0) + jnp.log(1.0 + jnp.exp(-jnp.abs(x)))


def _log_sigmoid(x):
    return jnp.minimum(x, 0.0) - jnp.log(1.0 + jnp.exp(-jnp.abs(x)))


def _sigmoid(x):
    return 1.0 / (1.0 + jnp.exp(-x))


def _head_norm_gate(o, norm_g, gate):
    ms = jnp.mean(o * o, axis=-1, keepdims=True)
    return o * lax.rsqrt(ms + EPS) * norm_g * (gate * _sigmoid(gate))


def _rmsnorm_kernel(x_ref, g_ref, o_ref):
    x = x_ref[...]
    ms = jnp.mean(x * x, axis=-1, keepdims=True)
    o_ref[...] = (x * lax.rsqrt(ms + EPS) * g_ref[...]).astype(o_ref.dtype)


def _rmsnorm(x, g, out_dtype):
    m, d = x.shape
    tm = _pick(m, (512, 256, 128, 64, 8))
    return pl.pallas_call(
        _rmsnorm_kernel,
        grid=(m // tm,),
        in_specs=[pl.BlockSpec((tm, d), lambda i: (i, 0)),
                  pl.BlockSpec((1, d), lambda i: (0, 0))],
        out_specs=pl.BlockSpec((tm, d), lambda i: (i, 0)),
        out_shape=jax.ShapeDtypeStruct((m, d), out_dtype),
        compiler_params=_params("parallel"),
        name="rmsnorm",
    )(x, g.reshape(1, d))


def _embed_norm_kernel(x_ref, p_ref, g_ref, h_ref, o_ref):
    x = jnp.where(pl.program_id(1) == 0, p_ref[...], x_ref[...])
    h_ref[...] = x
    ms = jnp.mean(x * x, axis=-1, keepdims=True)
    o_ref[...] = (x * lax.rsqrt(ms + EPS) * g_ref[...]).astype(o_ref.dtype)


def _embed_norm(x, prefix, g):
    bsz, n, d = x.shape
    tm = prefix.shape[0]
    assert n % tm == 0
    blk = lambda: pl.BlockSpec((None, tm, d), lambda b, i: (b, i, 0))
    return pl.pallas_call(
        _embed_norm_kernel,
        grid=(bsz, n // tm + 1),
        in_specs=[pl.BlockSpec((None, tm, d), lambda b, i: (b, jnp.maximum(i - 1, 0), 0)),
                  pl.BlockSpec((tm, d), lambda b, i: (0, 0)),
                  pl.BlockSpec((1, d), lambda b, i: (0, 0))],
        out_specs=[blk(), blk()],
        out_shape=[jax.ShapeDtypeStruct((bsz, n + tm, d), x.dtype),
                   jax.ShapeDtypeStruct((bsz, n + tm, d), BF16)],
        compiler_params=_params("parallel", "parallel"),
        name="embed_norm",
    )(x, prefix, g.reshape(1, d))


def _final_rmsnorm(h, g, n_skip):
    bsz, seq, d = h.shape
    tm = _pick(n_skip, (256, 128, 64, 8))
    assert (seq - n_skip) % tm == 0
    return pl.pallas_call(
        _rmsnorm_kernel,
        grid=(bsz, (seq - n_skip) // tm),
        in_specs=[pl.BlockSpec((None, tm, d), lambda b, i: (b, i + n_skip // tm, 0)),
                  pl.BlockSpec((1, d), lambda b, i: (0, 0))],
        out_specs=pl.BlockSpec((None, tm, d), lambda b, i: (b, i, 0)),
        out_shape=jax.ShapeDtypeStruct((bsz, seq - n_skip, d), h.dtype),
        compiler_params=_params("parallel", "parallel"),
        name="final_rmsnorm",
    )(h, g.reshape(1, d))


def _matmul_kernel(x_ref, w_ref, o_ref):
    o_ref[...] = _dot(x_ref[...], w_ref[...]).astype(o_ref.dtype)


def _matmul(x, w, out_dtype, name):
    m, k = x.shape
    n = w.shape[1]
    tm = _pick(m, (1536, 1408, 1024, 768, 512, 256, 128))
    tn = _pick(n, (512, 256, 128))
    return pl.pallas_call(
        _matmul_kernel,
        grid=(m // tm, n // tn),
        in_specs=[pl.BlockSpec((tm, k), lambda i, j: (i, 0)),
                  pl.BlockSpec((k, tn), lambda i, j: (0, j))],
        out_specs=pl.BlockSpec((tm, tn), lambda i, j: (i, j)),
        out_shape=jax.ShapeDtypeStruct((m, n), out_dtype),
        compiler_params=_params("parallel", "parallel"),
        name=name,
    )(x, w)


PROJ_TN = 512
CAST_ROWS = 512


def _cast_weight_tile(dst_ref, src_ref):
    step = min(CAST_ROWS, dst_ref.shape[0])

    def chunk(r, carry):
        rows = pl.ds(pl.multiple_of(r * step, step), step)
        dst_ref[rows, :] = src_ref[rows, :].astype(BF16)
        return carry

    lax.fori_loop(0, dst_ref.shape[0] // step, chunk, 0)


def _in_proj_kernel(x_ref, wt_ref, o_ref, w_ref):
    @pl.when(pl.program_id(1) == 0)
    def _():
        for c in range(wt_ref.shape[0] // LANE):
            cols = slice(c * LANE, (c + 1) * LANE)
            w_ref[:, cols] = wt_ref[cols, :].T.astype(BF16)

    o_ref[...] = _dot(x_ref[...], w_ref[...]).astype(o_ref.dtype)


def _in_proj(xn, w_t, row0, width, out_dtype, name):
    m, k = xn.shape
    tn = min(PROJ_TN, width)
    assert width % tn == 0 and row0 % SUBLANE == 0 and tn % LANE == 0
    tm = _pick(m, (1536, 1056, 768, 512, 256, 128))
    return pl.pallas_call(
        _in_proj_kernel,
        grid=(width // tn, m // tm),
        in_specs=[pl.BlockSpec((tm, k), lambda j, i: (i, 0)),
                  pl.BlockSpec((pl.Element(tn), pl.Element(k)),
                               lambda j, i: (pl.multiple_of(row0 + j * tn, SUBLANE), 0))],
        out_specs=pl.BlockSpec((tm, tn), lambda j, i: (i, j)),
        out_shape=jax.ShapeDtypeStruct((m, width), out_dtype),
        scratch_shapes=[pltpu.VMEM((k, tn), BF16)],
        compiler_params=_params("arbitrary", "arbitrary"),
        name=name,
    )(xn, w_t)


def _out_proj_kernel(a1_ref, a2_ref, a3_ref, wf_ref, h_ref, o_ref, w_ref):
    @pl.when(pl.program_id(1) == 0)
    def _():
        _cast_weight_tile(w_ref, wf_ref)

    k1, k2 = a1_ref.shape[1], a2_ref.shape[1]
    y = _dot(a1_ref[...], w_ref[0:k1, :])
    y = y + _dot(a2_ref[...], w_ref[k1:k1 + k2, :])
    y = y + _dot(a3_ref[...], w_ref[k1 + k2:, :])
    o_ref[...] = h_ref[...] + y


def _out_proj(a1, a2, a3, w_out, layer, h):
    m, d = h.shape
    k = w_out.shape[1]
    tn = PROJ_TN
    assert a1.shape[1] + a2.shape[1] + a3.shape[1] == k and d % tn == 0 and k % CAST_ROWS == 0
    tm = _pick(m, (1056, 768, 512, 256, 128))
    act = lambda a: pl.BlockSpec((tm, a.shape[1]), lambda j, i: (i, 0))
    return pl.pallas_call(
        _out_proj_kernel,
        grid=(d // tn, m // tm),
        in_specs=[act(a1), act(a2), act(a3),
                  pl.BlockSpec((None, k, tn), lambda j, i: (layer, 0, j)),
                  pl.BlockSpec((tm, tn), lambda j, i: (i, j))],
        out_specs=pl.BlockSpec((tm, tn), lambda j, i: (i, j)),
        out_shape=jax.ShapeDtypeStruct((m, d), F32),
        scratch_shapes=[pltpu.VMEM((k, tn), BF16)],
        compiler_params=_params("arbitrary", "arbitrary"),
        name="out_proj",
    )(a1, a2, a3, w_out, h)


def _gla_kernel(q_ref, k_ref, v_ref, r_ref, s_ref, wg_ref, bg_ref, ng_ref, o_ref, st_ref,
                *, n_chunks):
    @pl.when(pl.program_id(0) == 0)
    def _():
        st_ref[...] = jnp.zeros_like(st_ref)

    row = lax.broadcasted_iota(jnp.int32, (CHUNK, CHUNK), 0)
    col = lax.broadcasted_iota(jnp.int32, (CHUNK, CHUNK), 1)
    causal = row >= col
    tri = jnp.where(causal, 1.0, 0.0).astype(BF16)
    scale = GLA_DK ** -0.5
    wg_parts = _split_bf16(wg_ref[...], 2)
    ks = [slice(h * GLA_DK, (h + 1) * GLA_DK) for h in range(GLA_HEADS)]
    vs = [slice(h * GLA_DV, (h + 1) * GLA_DV) for h in range(GLA_HEADS)]
    pairs = [(n, h) for n in range(q_ref.shape[0]) for h in range(GLA_HEADS)]

    def chunk(c, carry):
        rows = pl.ds(pl.multiple_of(c * CHUNK, CHUNK), CHUNK)
        qt, kt, kl, dec = {}, {}, {}, {}
        for n in range(q_ref.shape[0]):
            x = _dot_split2(_split_bf16(s_ref[n, rows, :], 2), wg_parts) + bg_ref[...]
            g = _log_sigmoid(x) * (1.0 / GLA_TAU)
            b = _dot_exact_by_f32(tri, g)
            b_last = b[CHUNK - 1:CHUNK, :]
            q = q_ref[n, rows, :].astype(F32)
            k = k_ref[n, rows, :].astype(F32)
            qt[n] = (q * scale * jnp.exp(b)).astype(BF16)
            kt[n] = (k * jnp.exp(-b)).astype(BF16)
            kl[n] = (k * jnp.exp(b_last - b)).astype(BF16)
            dec[n] = jnp.exp(b_last)
        att = {(n, h): jnp.where(causal, _dot_nt(qt[n][:, ks[h]], kt[n][:, ks[h]]), 0.0).astype(BF16)
               for n, h in pairs}
        st = {(n, h): st_ref[n, h] for n, h in pairs}
        vh = {(n, h): v_ref[n, rows, vs[h]] for n, h in pairs}
        o = {(n, h): _dot(att[n, h], vh[n, h]) + _dot_nt(qt[n][:, ks[h]], st[n, h].astype(BF16))
             for n, h in pairs}
        for n, h in pairs:
            st_ref[n, h] = st[n, h] * dec[n][:, ks[h]] + _dot_tn(vh[n, h], kl[n][:, ks[h]])
        for n, h in pairs:
            gate = r_ref[n, rows, vs[h]].astype(F32)
            o_ref[n, rows, vs[h]] = _head_norm_gate(o[n, h], ng_ref[...], gate).astype(o_ref.dtype)
        return carry

    lax.fori_loop(0, n_chunks, chunk, 0, unroll=2)


def _gla(u_gla, u_small, w_gate_pad, b_gate, norm_g):
    bsz, seq, _ = u_gla.shape
    t = _pick(seq, (768, 384, 256, 128, 64))
    col = lambda off, w: pl.BlockSpec((bsz, t, w), lambda i: (0, i, off // w))
    full = lambda a: pl.BlockSpec(a.shape, lambda i: (0,) * a.ndim)
    return pl.pallas_call(
        functools.partial(_gla_kernel, n_chunks=t // CHUNK),
        grid=(seq // t,),
        in_specs=[col(OFF_GQ, GLA_QK), col(OFF_GK, GLA_QK), col(OFF_GV, GLA_W), col(OFF_GR, GLA_W),
                  pl.BlockSpec((bsz, t, LANE), lambda i: (0, i, 0)),
                  full(w_gate_pad), full(b_gate), full(norm_g)],
        out_specs=pl.BlockSpec((bsz, t, GLA_W), lambda i: (0, i, 0)),
        out_shape=jax.ShapeDtypeStruct((bsz, seq, GLA_W), BF16),
        scratch_shapes=[pltpu.VMEM((bsz, GLA_HEADS, GLA_DV, GLA_DK), F32)],
        compiler_params=_params("arbitrary"),
        name="gla",
    )(u_gla, u_gla, u_gla, u_gla, u_small, w_gate_pad, b_gate, norm_g)


GDN_PREP_T = 128
HALO = 16


def _gdn_prep_kernel(q_ref, k_ref, v_ref, qh_ref, kh_ref, vh_ref, s_ref, cw_ref, al_ref, dt_ref,
                     eb_ref, eg_ref, qg_ref, kd_ref, u_ref, w_ref, aqk_ref, gl_ref,
                     qs_ref, ks_ref, vs_ref):
    first = pl.program_id(1) == 0
    t = q_ref.shape[0]

    out_r = lax.broadcasted_iota(jnp.int32, ((CONV_K - 1) * t, HALO + t), 0)
    in_r = lax.broadcasted_iota(jnp.int32, ((CONV_K - 1) * t, HALO + t), 1)
    t_log2 = t.bit_length() - 1
    assert 1 << t_log2 == t
    shift_op = jnp.where(in_r == HALO + (out_r & (t - 1)) - ((out_r >> t_log2) + 1),
                         1.0, 0.0).astype(BF16)

    def conv_silu(x_ref, h_ref, w):
        x = x_ref[...]
        hist = jnp.where(first, jnp.zeros_like(h_ref[...]), h_ref[...])
        taps = _dot(shift_op, jnp.concatenate([hist, x], axis=0))
        y = x.astype(F32) * w[CONV_K - 1:CONV_K, :]
        for j in range(1, CONV_K):
            y = y + taps[(j - 1) * t:j * t, :] * w[CONV_K - 1 - j:CONV_K - j, :]
        return y * _sigmoid(y)

    qc = conv_silu(q_ref, qh_ref, cw_ref[:, 0:GDN_QK])
    kc = conv_silu(k_ref, kh_ref, cw_ref[:, GDN_QK:2 * GDN_QK])
    vs_ref[...] = conv_silu(v_ref, vh_ref, cw_ref[:, 2 * GDN_QK:2 * GDN_QK + GDN_W])
    for h in range(GDN_HEADS):
        sl = slice(h * GDN_DK, (h + 1) * GDN_DK)
        qh = qc[:, sl]
        kh = kc[:, sl]
        qs_ref[:, sl] = qh * (lax.rsqrt(jnp.sum(qh * qh, axis=-1, keepdims=True) + EPS)
                              * GDN_DK ** -0.5)
        ks_ref[:, sl] = kh * lax.rsqrt(jnp.sum(kh * kh, axis=-1, keepdims=True) + EPS)

    s = s_ref[...]
    beta = _sigmoid(s)
    g = -jnp.exp(al_ref[...]) * _softplus(s + dt_ref[...])

    row = lax.broadcasted_iota(jnp.int32, (CHUNK, CHUNK), 0)
    col = lax.broadcasted_iota(jnp.int32, (CHUNK, CHUNK), 1)
    causal = row >= col
    strict = row > col
    tri = jnp.where(causal, 1.0, 0.0).astype(BF16)
    zeros_half = jnp.zeros((CHUNK, GDN_DV - CHUNK), F32)

    pairs = [(c, h) for c in range(t // CHUNK) for h in range(GDN_HEADS)]
    a_neg, rhs = {}, {}
    for c in range(t // CHUNK):
        rows = slice(c * CHUNK, (c + 1) * CHUNK)
        gcum = _dot_exact_by_f32(tri, g[rows])
        gcum_t = gcum.T
        gb = _dot_f32_by_exact(gcum, eg_ref[...])
        bb = _dot_f32_by_exact(beta[rows], eb_ref[...])
        g_last = gb[CHUNK - 1:CHUNK, :]
        eg = jnp.exp(gb)
        q = qs_ref[rows, :]
        k = ks_ref[rows, :]
        kb = k * bb
        qg_ref[rows, :] = (q * eg).astype(BF16)
        kd_ref[rows, :] = (k * jnp.exp(g_last - gb)).astype(BF16)
        gl_ref[c * SUBLANE:(c + 1) * SUBLANE, :] = jnp.broadcast_to(jnp.exp(g_last), (SUBLANE, GDN_W))
        rhs_v = vs_ref[rows, :] * bb
        rhs_k = kb * eg
        for h in range(GDN_HEADS):
            sl = slice(h * GDN_DK, (h + 1) * GDN_DK)
            lhs = jnp.concatenate([kb[:, sl], q[:, sl]], axis=0).astype(BF16)
            pr = _dot_nt(lhs, k[:, sl].astype(BF16))
            g_t = gb[:, h * GDN_DK:h * GDN_DK + CHUNK]
            g_s = gcum_t[S_A + h:S_A + h + 1, :]
            dec = jnp.exp(jnp.where(causal, g_t - g_s, -jnp.inf))
            a_neg[c, h] = jnp.where(strict, -pr[:CHUNK] * dec, 0.0)
            aqk_ref[rows, sl] = jnp.concatenate([pr[CHUNK:] * dec, zeros_half], axis=1).astype(BF16)
            rhs[c, h] = jnp.concatenate([rhs_v[:, sl], rhs_k[:, sl]], axis=1)
    toff = dict(a_neg)
    p = dict(a_neg)
    for _ in range(5):
        for key in pairs:
            pb = p[key].astype(BF16)
            p[key] = _dot(pb, pb)
        for key in pairs:
            toff[key] = toff[key] + p[key] + _dot(toff[key].astype(BF16), p[key].astype(BF16))
    for c, h in pairs:
        rows = slice(c * CHUNK, (c + 1) * CHUNK)
        sl = slice(h * GDN_DK, (h + 1) * GDN_DK)
        sol = rhs[c, h] + _dot(toff[c, h].astype(BF16), rhs[c, h].astype(BF16))
        u_ref[rows, sl] = sol[:, :GDN_DV]
        w_ref[rows, sl] = sol[:, GDN_DV:].astype(BF16)


def _gdn_scan_kernel(qg_ref, kd_ref, u_ref, w_ref, aqk_ref, gl_ref, z_ref, ng_ref, o_ref, st_ref,
                     *, n_chunks):
    @pl.when(pl.program_id(0) == 0)
    def _():
        st_ref[...] = jnp.zeros_like(st_ref)

    sls = [slice(h * GDN_DK, (h + 1) * GDN_DK) for h in range(GDN_HEADS)]
    pairs = [(n, h) for n in range(qg_ref.shape[0]) for h in range(GDN_HEADS)]

    def chunk(c, carry):
        rows = pl.ds(pl.multiple_of(c * CHUNK, CHUNK), CHUNK)
        grow = pl.ds(pl.multiple_of(c * SUBLANE, SUBLANE), SUBLANE)
        st = {(n, h): st_ref[n, h] for n, h in pairs}
        ws = {(n, h): _dot(jnp.concatenate([w_ref[n, rows, sls[h]], qg_ref[n, rows, sls[h]]], axis=0),
                           st[n, h].astype(BF16)) for n, h in pairs}
        vn = {(n, h): (u_ref[n, rows, sls[h]] - ws[n, h][:CHUNK]).astype(BF16) for n, h in pairs}
        o = {(n, h): ws[n, h][CHUNK:] + _dot(aqk_ref[n, rows, h * GDN_DK:h * GDN_DK + CHUNK], vn[n, h])
             for n, h in pairs}
        for n, h in pairs:
            st_ref[n, h] = (st[n, h] * gl_ref[n, grow, sls[h]][0:1, :]
                            + _dot_tn(kd_ref[n, rows, sls[h]], vn[n, h]))
        for n, h in pairs:
            gate = z_ref[n, rows, sls[h]].astype(F32)
            o_ref[n, rows, sls[h]] = _head_norm_gate(o[n, h], ng_ref[...], gate).astype(o_ref.dtype)
        return carry

    lax.fori_loop(0, n_chunks, chunk, 0)


def _gdn(u_main, u_small, conv_w, a_log_row, dt_row, e_beta, e_g, norm_g):
    bsz, seq, _ = u_main.shape
    t = GDN_PREP_T
    assert seq % t == 0 and t % HALO == 0
    blk = lambda off: pl.BlockSpec((None, t, GDN_W), lambda b, i: (b, i, off // GDN_W))
    halo = lambda off: pl.BlockSpec(
        (None, HALO, GDN_W), lambda b, i: (b, jnp.maximum(i * (t // HALO) - 1, 0), off // GDN_W))
    full = lambda a: pl.BlockSpec(a.shape, lambda b, i: (0,) * a.ndim)
    out = lambda: pl.BlockSpec((None, t, GDN_W), lambda b, i: (b, i, 0))
    gl_rows = t // CHUNK * SUBLANE
    sds = lambda dt: jax.ShapeDtypeStruct((bsz, seq, GDN_W), dt)
    qg, kd, u, w, aqk, gl = pl.pallas_call(
        _gdn_prep_kernel,
        grid=(bsz, seq // t),
        in_specs=[blk(OFF_DQ), blk(OFF_DK), blk(OFF_DV), halo(OFF_DQ), halo(OFF_DK), halo(OFF_DV),
                  pl.BlockSpec((None, t, LANE), lambda b, i: (b, i, 0)),
                  full(conv_w), full(a_log_row), full(dt_row), full(e_beta), full(e_g)],
        out_specs=[out(), out(), out(), out(), out(),
                   pl.BlockSpec((None, gl_rows, GDN_W), lambda b, i: (b, i, 0))],
        out_shape=[sds(BF16), sds(BF16), sds(F32), sds(BF16), sds(BF16),
                   jax.ShapeDtypeStruct((bsz, seq // CHUNK * SUBLANE, GDN_W), F32)],
        scratch_shapes=[pltpu.VMEM((t, GDN_W), F32)] * 3,
        compiler_params=_params("parallel", "parallel"),
        name="gdn_prep",
    )(u_main, u_main, u_main, u_main, u_main, u_main, u_small, conv_w, a_log_row, dt_row, e_beta, e_g)

    ts = _pick(seq, (256, 128, 64))
    blk = lambda: pl.BlockSpec((bsz, ts, GDN_W), lambda i: (0, i, 0))
    return pl.pallas_call(
        functools.partial(_gdn_scan_kernel, n_chunks=ts // CHUNK),
        grid=(seq // ts,),
        in_specs=[blk(), blk(), blk(), blk(), blk(),
                  pl.BlockSpec((bsz, ts // CHUNK * SUBLANE, GDN_W), lambda i: (0, i, 0)),
                  pl.BlockSpec((bsz, ts, GDN_W), lambda i: (0, i, OFF_DZ // GDN_W)),
                  pl.BlockSpec(norm_g.shape, lambda i: (0, 0))],
        out_specs=blk(),
        out_shape=jax.ShapeDtypeStruct((bsz, seq, GDN_W), BF16),
        scratch_shapes=[pltpu.VMEM((bsz, GDN_HEADS, GDN_DK, GDN_DV), F32)],
        compiler_params=_params("arbitrary"),
        name="gdn_scan",
    )(qg, kd, u, w, aqk, gl, u_main, norm_g)


SB_RB = 128
SB_LOOK = 2
SB_QT = 2 * SB_RB
SB_EXP_ZERO = 104.0


def _sb_kernel(q_ref, k_ref, v_ref, g_ref, ng_ref, o_ref, acc_ref, run_ref, z_buf, hl_buf, a_buf,
               *, n_invalid):
    seq = q_ref.shape[0]
    rb = SB_RB
    scale = SB_D ** -0.5
    jr = lax.broadcasted_iota(jnp.int32, (2 * rb, 2 * rb), 0)
    sc = lax.broadcasted_iota(jnp.int32, (2 * rb, 2 * rb), 1)
    rr = jnp.where((sc >= rb) | ((jr & (rb - 1)) >= sc), 1.0, 0.0).astype(BF16)
    row_i = lax.broadcasted_iota(jnp.int32, (rb, rb), 0)
    col_i = lax.broadcasted_iota(jnp.int32, (rb, rb), 1)
    below_diag = col_i < row_i

    def softplus(z):
        return jnp.maximum(z, 0.0) + jnp.log(1.0 + jnp.exp2(jnp.abs(z) * -LOG2E))

    def weights(z, sp, run):
        hi = sp.astype(BF16)
        lo = (sp - hi.astype(F32)).astype(BF16)
        cs = _dot(jnp.concatenate([hi, lo], axis=1), rr)
        return jnp.exp(z - cs[:, :rb] - run).astype(BF16), cs[:, rb:]

    def mask(z, sp, vis):
        return jnp.where(vis, z, -jnp.inf), jnp.where(vis, sp, 0.0)

    def finish(rows, acc):
        gate = g_ref[rows, :].astype(F32)
        o_ref[rows, :] = _head_norm_gate(acc, ng_ref[...], gate).astype(o_ref.dtype)

    def row_block(zs, key0, check_valid):
        run = jnp.zeros((rb, rb), F32)
        parts = [None] * len(zs)
        for c in reversed(range(len(zs))):
            z, sp = zs[c], softplus(zs[c])
            vis = below_diag if c == len(zs) - 1 else None
            if check_valid:
                ok = (key0 + c * rb + col_i) >= n_invalid
                vis = ok if vis is None else (vis & ok)
            if vis is not None:
                z, sp = mask(z, sp, vis)
            parts[c], tot = weights(z, sp, run)
            run = run + tot
        return jnp.concatenate(parts, axis=1), run

    def qtile(r0, looks, check_valid):
        k0 = r0 - looks[0] * rb
        nk = looks[0] + 2
        zz = _dot_nt(q_ref[pl.ds(r0, SB_QT), :], k_ref[pl.ds(k0, nk * rb), :]) * scale
        run_min = None
        for r in range(2):
            c0 = r - looks[r] + looks[0]
            zs = [zz[r * rb:(r + 1) * rb, c * rb:(c + 1) * rb] for c in range(c0, c0 + looks[r] + 1)]
            a, run = row_block(zs, k0 + c0 * rb, check_valid)
            acc = _dot(a, v_ref[pl.ds(k0 + c0 * rb, (looks[r] + 1) * rb), :])
            rows = pl.ds(r0 + r * rb, rb)
            acc_ref[rows, :] = acc
            run_ref[rows, :] = run
            finish(rows, acc)
            run_min = run if run_min is None else jnp.minimum(run_min, run)
        return run_min

    first_open = n_invalid // rb + SB_LOOK + 1
    assert first_open % 2 == 0 and first_open * rb <= seq and (first_open - SB_LOOK) * rb >= n_invalid
    qtile(0, (0, 1), True)
    for qi in range(1, first_open // 2):
        qtile(qi * SB_QT, (SB_LOOK, SB_LOOK), True)

    n_sub = SB_LOOK + 1
    t0 = first_open // 2
    n_tiles = seq // SB_QT - t0

    def tile_row(t):
        r0 = (t0 + t) * SB_QT
        return r0 if isinstance(t, int) else pl.multiple_of(r0, SB_QT)

    def scores(t):
        r0 = tile_row(t)
        k0 = r0 - SB_LOOK * rb
        zz = _dot_nt(q_ref[pl.ds(r0, SB_QT), :], k_ref[pl.ds(k0, (SB_LOOK + 2) * rb), :]) * scale
        for r in range(2):
            for c in range(n_sub):
                z = zz[r * rb:(r + 1) * rb, (r + c) * rb:(r + c + 1) * rb]
                sp = softplus(z)
                if c == n_sub - 1:
                    z, sp = mask(z, sp, below_diag)
                hi = sp.astype(BF16)
                lo = (sp - hi.astype(F32)).astype(BF16)
                z_buf[r, c] = z
                hl_buf[r, c] = jnp.concatenate([hi, lo], axis=1)

    def weights_of(t):
        r0 = tile_row(t)
        run_min = None
        for r in range(2):
            run = jnp.zeros((rb, rb), F32)
            for c in reversed(range(n_sub)):
                cs = _dot(hl_buf[r, c], rr)
                a_buf[r, :, c * rb:(c + 1) * rb] = jnp.exp(z_buf[r, c] - cs[:, :rb] - run).astype(BF16)
                run = run + cs[:, rb:]
            run_ref[pl.ds(r0 + r * rb, rb), :] = run
            run_min = run if run_min is None else jnp.minimum(run_min, run)
        return run_min

    def values(t):
        r0 = tile_row(t)
        k0 = r0 - SB_LOOK * rb
        for r in range(2):
            acc = _dot(a_buf[r], v_ref[pl.ds(k0 + r * rb, n_sub * rb), :])
            rows = pl.ds(r0 + r * rb, rb)
            acc_ref[rows, :] = acc
            finish(rows, acc)

    def step(s, run_min, stages):
        if stages[2]:
            values(s - 2)
        if stages[1]:
            run_min = jnp.minimum(run_min, weights_of(s - 1))
        if stages[0]:
            scores(s)
        return run_min

    run_min = jnp.full((rb, rb), 2 * SB_EXP_ZERO, F32)
    for s in range(n_tiles + 2):
        if s == 2 and n_tiles > 2:
            run_min = lax.fori_loop(2, n_tiles, lambda t, m: step(t, m, (True, True, True)), run_min,
                                    unroll=2)
        if 2 <= s < n_tiles:
            continue
        run_min = step(s, run_min, [0 <= s - k < n_tiles for k in (0, 1, 2)])

    @pl.when(jnp.min(run_min) < SB_EXP_ZERO)
    def _():
        def second_pass(b, carry):
            rows = pl.ds(pl.multiple_of(b * rb, rb), rb)

            def unfinished(state):
                j, lowest = state
                return (j >= 0) & (lowest < SB_EXP_ZERO)

            def sub_block(state):
                j, _ = state
                keys = pl.ds(pl.multiple_of(j * rb, rb), rb)
                z = _dot_nt(q_ref[rows, :], k_ref[keys, :]) * scale
                z, sp = mask(z, softplus(z), (j * rb + col_i) >= n_invalid)
                run = run_ref[rows, :]
                a, tot = weights(z, sp, run)
                acc_ref[rows, :] += _dot(a, v_ref[keys, :])
                run_ref[rows, :] = run + tot
                return j - 1, jnp.min(run + tot)

            lax.while_loop(unfinished, sub_block, (b - SB_LOOK - 1, jnp.min(run_ref[rows, :])))
            finish(rows, acc_ref[rows, :])
            return carry

        lax.fori_loop(first_open, seq // rb, second_pass, 0)


def _sb(u_main, norm_g, n_invalid):
    bsz, seq, _ = u_main.shape
    assert seq % SB_QT == 0 and SB_D == SB_RB
    blk = lambda off: pl.BlockSpec((None, seq, SB_D), lambda b, h: (b, 0, off // SB_D + h))
    return pl.pallas_call(
        functools.partial(_sb_kernel, n_invalid=n_invalid),
        grid=(bsz, SB_HEADS),
        in_specs=[blk(OFF_SQ), blk(OFF_SK), blk(OFF_SV), blk(OFF_SG),
                  pl.BlockSpec(norm_g.shape, lambda b, h: (0, 0))],
        out_specs=pl.BlockSpec((None, seq, SB_D), lambda b, h: (b, 0, h)),
        out_shape=jax.ShapeDtypeStruct((bsz, seq, SB_W), BF16),
        scratch_shapes=[pltpu.VMEM((seq, SB_D), F32), pltpu.VMEM((seq, SB_D), F32),
                        pltpu.VMEM((2, SB_LOOK + 1, SB_RB, SB_RB), F32),
                        pltpu.VMEM((2, SB_LOOK + 1, SB_RB, 2 * SB_RB), BF16),
                        pltpu.VMEM((2, SB_RB, (SB_LOOK + 1) * SB_RB), BF16)],
        compiler_params=_params("parallel", "parallel"),
        name="stick_breaking",
    )(u_main, u_main, u_main, u_main, norm_g)


def _place(vec, lane0):
    return jnp.zeros((1, LANE), F32).at[0, lane0:lane0 + vec.shape[0]].set(vec.astype(F32))


def _expand(lane0, heads, width):
    src = jnp.arange(LANE)[:, None]
    dst_head = jnp.arange(heads * width)[None, :] // width
    return (src == lane0 + dst_head).astype(BF16)


def _layer(h, xn, w_t, w_out, layer, gla_w_gate, gla_b_gate, gla_norm_g, gdn_conv_w, gdn_a_log,
           gdn_dt_bias, gdn_norm_g, sb_norm_g, bsz, seq):
    m = bsz * seq
    o = _IN_OFF
    base = layer * o[-1]
    w_small = jnp.concatenate(
        [w_t[base + o[4]:base + o[5]], w_t[base + o[9]:base + o[11]],
         jnp.zeros((LANE - GLA_RANK - 2 * GDN_HEADS, w_t.shape[1]), w_t.dtype)], axis=0)

    def project(lo, hi, name):
        return _in_proj(xn, w_t, base + lo, hi - lo, BF16, name).reshape(bsz, seq, hi - lo)

    u_gla = project(o[0], o[4], "in_proj_gla")
    u_gdn = project(o[5], o[9], "in_proj_gdn")
    u_sb = project(o[11], o[15], "in_proj_sb")
    u_small = _in_proj(xn, w_small, 0, LANE, F32, "in_proj_small").reshape(bsz, seq, LANE)

    w_gate_pad = jnp.zeros((LANE, GLA_QK), F32).at[S_LR:S_LR + GLA_RANK].set(gla_w_gate)
    o_gla = _gla(u_gla, u_small, w_gate_pad, gla_b_gate.reshape(1, GLA_QK),
                 gla_norm_g.reshape(1, GLA_DV))
    o_gdn = _gdn(u_gdn, u_small, gdn_conv_w, _place(gdn_a_log, S_A), _place(gdn_dt_bias, S_A),
                 _expand(S_B, GDN_HEADS, GDN_DK), _expand(S_A, GDN_HEADS, GDN_DK),
                 gdn_norm_g.reshape(1, GDN_DV))
    o_sb = _sb(u_sb, sb_norm_g.reshape(1, SB_D), PREFIX - N_META)

    return _out_proj(o_gla.reshape(m, GLA_W), o_gdn.reshape(m, GDN_W), o_sb.reshape(m, SB_W),
                     w_out, layer, h)


def kernel(x, meta, norm_g, w_in, gla_w_gate, gla_b_gate, gla_norm_g, gdn_conv_w, gdn_a_log,
           gdn_dt_bias, gdn_norm_g, sb_norm_g, w_out, final_g):
    bsz, n, d = x.shape
    prefix = jnp.concatenate([jnp.zeros((PREFIX - N_META, d), x.dtype), meta.astype(x.dtype)], axis=0)
    seq = PREFIX + n
    h, xn = _embed_norm(x, prefix, norm_g[0])
    h = h.reshape(bsz * seq, d)
    xn = xn.reshape(bsz * seq, d)
    w_t = jnp.swapaxes(w_in, 1, 2).reshape(-1, d)
    for l in range(norm_g.shape[0]):
        if l > 0:
            xn = _rmsnorm(h, norm_g[l], BF16)
        h = _layer(h, xn, w_t, w_out, l, gla_w_gate[l], gla_b_gate[l], gla_norm_g[l], gdn_conv_w[l],
                   gdn_a_log[l], gdn_dt_bias[l], gdn_norm_g[l], sb_norm_g[l], bsz, seq)
    return _final_rmsnorm(h.reshape(bsz, seq, d), final_g, PREFIX)
```

```python
import functools

import jax
import jax.numpy as jnp
from jax import lax
from jax.experimental import pallas as pl
from jax.experimental.pallas import tpu as pltpu

F32 = jnp.float32
BF16 = jnp.bfloat16
HIGHEST = lax.Precision.HIGHEST

N_META = 16
PREFIX = 256
CHUNK = 64
EPS = 1e-6
LOG2E = 1.4426950408889634

GLA_HEADS, GLA_DK, GLA_DV, GLA_RANK, GLA_TAU = 4, 128, 256, 16, 16.0
GDN_HEADS, GDN_DK, GDN_DV, CONV_K = 12, 128, 128, 4
SB_HEADS, SB_D = 12, 128

GLA_QK = GLA_HEADS * GLA_DK
GLA_W = GLA_HEADS * GLA_DV
GDN_QK = GDN_HEADS * GDN_DK
GDN_W = GDN_HEADS * GDN_DV
SB_W = SB_HEADS * SB_D

_IN_SPLITS = (GLA_QK, GLA_QK, GLA_W, GLA_W, GLA_RANK,
              GDN_QK, GDN_QK, GDN_W, GDN_W, GDN_HEADS, GDN_HEADS,
              SB_W, SB_W, SB_W, SB_W)
_IN_OFF = [0]
for _w in _IN_SPLITS:
    _IN_OFF.append(_IN_OFF[-1] + _w)

OFF_GQ, OFF_GK, OFF_GV, OFF_GR = 0, 512, 1024, 2048
OFF_DQ, OFF_DK, OFF_DV, OFF_DZ = 0, 1536, 3072, 4608
OFF_SQ, OFF_SK, OFF_SV, OFF_SG = 0, 1536, 3072, 4608
LANE = 128
SUBLANE = 8
S_LR, S_B, S_A = 0, GLA_RANK, GLA_RANK + GDN_HEADS

V7X_VMEM_LIMIT_BYTES = 56 * 1024 * 1024


def _pick(n, candidates):
    for c in candidates:
        if n % c == 0:
            return c
    raise ValueError(f"no block size in {candidates} divides {n}")


def _params(*sem):
    return pltpu.CompilerParams(dimension_semantics=sem, vmem_limit_bytes=V7X_VMEM_LIMIT_BYTES)


def _dot(a, b):
    return jnp.dot(a, b, preferred_element_type=F32)


def _dot_nt(a, b):
    return lax.dot_general(a, b, (((1,), (1,)), ((), ())), preferred_element_type=F32)


def _dot_tn(a, b):
    return lax.dot_general(a, b, (((0,), (0,)), ((), ())), preferred_element_type=F32)


def _split_bf16(x, n):
    parts = []
    for _ in range(n):
        p = x.astype(BF16)
        parts.append(p)
        x = x - p.astype(F32)
    return parts


def _dot_f32_by_exact(a, b_exact):
    hi, mid, lo = _split_bf16(a, 3)
    return _dot(hi, b_exact) + _dot(mid, b_exact) + _dot(lo, b_exact)


def _dot_exact_by_f32(a_exact, b):
    hi, mid, lo = _split_bf16(b, 3)
    return _dot(a_exact, hi) + _dot(a_exact, mid) + _dot(a_exact, lo)


def _dot_split2(a_parts, b_parts):
    return _dot(a_parts[0], b_parts[0]) + _dot(a_parts[0], b_parts[1]) + _dot(a_parts[1], b_parts[0])


def _softplus(x):
    return jnp.maximum(x, 0.0) + jnp.log(1.0 + jnp.exp(-jnp.abs(x)))


def _log_sigmoid(x):
    return jnp.minimum(x, 0.0) - jnp.log(1.0 + jnp.exp(-jnp.abs(x)))


def _sigmoid(x):
    return 1.0 / (1.0 + jnp.exp(-x))


def _head_norm_gate(o, norm_g, gate):
    ms = jnp.mean(o * o, axis=-1, keepdims=True)
    return o * lax.rsqrt(ms + EPS) * norm_g * (gate * _sigmoid(gate))


def _rmsnorm_kernel(x_ref, g_ref, o_ref):
    x = x_ref[...]
    ms = jnp.mean(x * x, axis=-1, keepdims=True)
    o_ref[...] = (x * lax.rsqrt(ms + EPS) * g_ref[...]).astype(o_ref.dtype)


def _norm_and_narrow(x, g_ref, ws_ref, o_ref, s_ref):
    ms = jnp.mean(x * x, axis=-1, keepdims=True)
    xn = (x * lax.rsqrt(ms + EPS) * g_ref[...]).astype(BF16)
    o_ref[...] = xn
    s_ref[...] = _dot_nt(xn, ws_ref[...].astype(BF16))


def _rmsnorm_narrow_kernel(x_ref, g_ref, ws_ref, o_ref, s_ref):
    _norm_and_narrow(x_ref[...], g_ref, ws_ref, o_ref, s_ref)


def _rmsnorm_narrow(x, g, w_small):
    m, d = x.shape
    tm = _pick(m, (512, 256, 128, 64, 8))
    return pl.pallas_call(
        _rmsnorm_narrow_kernel,
        grid=(m // tm,),
        in_specs=[pl.BlockSpec((tm, d), lambda i: (i, 0)),
                  pl.BlockSpec((1, d), lambda i: (0, 0)),
                  pl.BlockSpec(w_small.shape, lambda i: (0, 0))],
        out_specs=[pl.BlockSpec((tm, d), lambda i: (i, 0)),
                   pl.BlockSpec((tm, LANE), lambda i: (i, 0))],
        out_shape=[jax.ShapeDtypeStruct((m, d), BF16), jax.ShapeDtypeStruct((m, LANE), F32)],
        compiler_params=_params("parallel"),
        name="rmsnorm",
    )(x, g.reshape(1, d), w_small)


def _embed_norm_kernel(x_ref, p_ref, g_ref, ws_ref, h_ref, o_ref, s_ref):
    x = jnp.where(pl.program_id(1) == 0, p_ref[...], x_ref[...])
    h_ref[...] = x
    _norm_and_narrow(x, g_ref, ws_ref, o_ref, s_ref)


def _embed_norm(x, prefix, g, w_small):
    bsz, n, d = x.shape
    tm = prefix.shape[0]
    assert n % tm == 0
    blk = lambda w: pl.BlockSpec((None, tm, w), lambda b, i: (b, i, 0))
    return pl.pallas_call(
        _embed_norm_kernel,
        grid=(bsz, n // tm + 1),
        in_specs=[pl.BlockSpec((None, tm, d), lambda b, i: (b, jnp.maximum(i - 1, 0), 0)),
                  pl.BlockSpec((tm, d), lambda b, i: (0, 0)),
                  pl.BlockSpec((1, d), lambda b, i: (0, 0)),
                  pl.BlockSpec(w_small.shape, lambda b, i: (0, 0))],
        out_specs=[blk(d), blk(d), blk(LANE)],
        out_shape=[jax.ShapeDtypeStruct((bsz, n + tm, d), x.dtype),
                   jax.ShapeDtypeStruct((bsz, n + tm, d), BF16),
                   jax.ShapeDtypeStruct((bsz, n + tm, LANE), F32)],
        compiler_params=_params("parallel", "parallel"),
        name="embed_norm",
    )(x, prefix, g.reshape(1, d), w_small)


def _final_rmsnorm(h, g, n_skip):
    bsz, seq, d = h.shape
    tm = _pick(n_skip, (256, 128, 64, 8))
    assert (seq - n_skip) % tm == 0
    return pl.pallas_call(
        _rmsnorm_kernel,
        grid=(bsz, (seq - n_skip) // tm),
        in_specs=[pl.BlockSpec((None, tm, d), lambda b, i: (b, i + n_skip // tm, 0)),
                  pl.BlockSpec((1, d), lambda b, i: (0, 0))],
        out_specs=pl.BlockSpec((None, tm, d), lambda b, i: (b, i, 0)),
        out_shape=jax.ShapeDtypeStruct((bsz, seq - n_skip, d), h.dtype),
        compiler_params=_params("parallel", "parallel"),
        name="final_rmsnorm",
    )(h, g.reshape(1, d))


def _matmul_kernel(x_ref, w_ref, o_ref):
    o_ref[...] = _dot(x_ref[...], w_ref[...]).astype(o_ref.dtype)


def _matmul(x, w, out_dtype, name):
    m, k = x.shape
    n = w.shape[1]
    tm = _pick(m, (1536, 1408, 1024, 768, 512, 256, 128))
    tn = _pick(n, (512, 256, 128))
    return pl.pallas_call(
        _matmul_kernel,
        grid=(m // tm, n // tn),
        in_specs=[pl.BlockSpec((tm, k), lambda i, j: (i, 0)),
                  pl.BlockSpec((k, tn), lambda i, j: (0, j))],
        out_specs=pl.BlockSpec((tm, tn), lambda i, j: (i, j)),
        out_shape=jax.ShapeDtypeStruct((m, n), out_dtype),
        compiler_params=_params("parallel", "parallel"),
        name=name,
    )(x, w)


PROJ_TN = 512
CAST_ROWS = 512


def _cast_weight_tile(dst_ref, src_ref):
    step = min(CAST_ROWS, dst_ref.shape[0])

    def chunk(r, carry):
        rows = pl.ds(pl.multiple_of(r * step, step), step)
        dst_ref[rows, :] = src_ref[rows, :].astype(BF16)
        return carry

    lax.fori_loop(0, dst_ref.shape[0] // step, chunk, 0)


def _in_proj_kernel(x_ref, wt_ref, o_ref, w_ref):
    @pl.when(pl.program_id(1) == 0)
    def _():
        for c in range(wt_ref.shape[0] // LANE):
            cols = slice(c * LANE, (c + 1) * LANE)
            w_ref[:, cols] = wt_ref[cols, :].T.astype(BF16)

    o_ref[...] = _dot(x_ref[...], w_ref[...]).astype(o_ref.dtype)


def _in_proj(xn, w_t, row0, width, out_dtype, name):
    m, k = xn.shape
    tn = min(PROJ_TN, width)
    assert width % tn == 0 and row0 % SUBLANE == 0 and tn % LANE == 0
    tm = _pick(m, (1536, 1056, 768, 512, 256, 128))
    return pl.pallas_call(
        _in_proj_kernel,
        grid=(width // tn, m // tm),
        in_specs=[pl.BlockSpec((tm, k), lambda j, i: (i, 0)),
                  pl.BlockSpec((pl.Element(tn), pl.Element(k)),
                               lambda j, i: (pl.multiple_of(row0 + j * tn, SUBLANE), 0))],
        out_specs=pl.BlockSpec((tm, tn), lambda j, i: (i, j)),
        out_shape=jax.ShapeDtypeStruct((m, width), out_dtype),
        scratch_shapes=[pltpu.VMEM((k, tn), BF16)],
        compiler_params=_params("arbitrary", "arbitrary"),
        name=name,
    )(xn, w_t)


def _out_proj_kernel(a1_ref, a2_ref, a3_ref, wf_ref, h_ref, o_ref, w_ref):
    @pl.when(pl.program_id(1) == 0)
    def _():
        _cast_weight_tile(w_ref, wf_ref)

    k1, k2 = a1_ref.shape[1], a2_ref.shape[1]
    y = _dot(a1_ref[...], w_ref[0:k1, :])
    y = y + _dot(a2_ref[...], w_ref[k1:k1 + k2, :])
    y = y + _dot(a3_ref[...], w_ref[k1 + k2:, :])
    o_ref[...] = h_ref[...] + y


def _out_proj(a1, a2, a3, w_out, layer, h):
    m, d = h.shape
    k = w_out.shape[1]
    tn = PROJ_TN
    assert a1.shape[1] + a2.shape[1] + a3.shape[1] == k and d % tn == 0 and k % CAST_ROWS == 0
    tm = _pick(m, (1056, 768, 512, 256, 128))
    act = lambda a: pl.BlockSpec((tm, a.shape[1]), lambda j, i: (i, 0))
    return pl.pallas_call(
        _out_proj_kernel,
        grid=(d // tn, m // tm),
        in_specs=[act(a1), act(a2), act(a3),
                  pl.BlockSpec((None, k, tn), lambda j, i: (layer, 0, j)),
                  pl.BlockSpec((tm, tn), lambda j, i: (i, j))],
        out_specs=pl.BlockSpec((tm, tn), lambda j, i: (i, j)),
        out_shape=jax.ShapeDtypeStruct((m, d), F32),
        scratch_shapes=[pltpu.VMEM((k, tn), BF16)],
        compiler_params=_params("arbitrary", "arbitrary"),
        name="out_proj",
    )(a1, a2, a3, w_out, h)


def _gla_kernel(q_ref, k_ref, v_ref, r_ref, s_ref, wg_ref, bg_ref, ng_ref, o_ref, st_ref,
                *, n_chunks):
    @pl.when(pl.program_id(0) == 0)
    def _():
        st_ref[...] = jnp.zeros_like(st_ref)

    row = lax.broadcasted_iota(jnp.int32, (CHUNK, CHUNK), 0)
    col = lax.broadcasted_iota(jnp.int32, (CHUNK, CHUNK), 1)
    causal = row >= col
    tri = jnp.where(causal, 1.0, 0.0).astype(BF16)
    scale = GLA_DK ** -0.5
    wg_parts = _split_bf16(wg_ref[...], 2)
    ks = [slice(h * GLA_DK, (h + 1) * GLA_DK) for h in range(GLA_HEADS)]
    vs = [slice(h * GLA_DV, (h + 1) * GLA_DV) for h in range(GLA_HEADS)]
    pairs = [(n, h) for n in range(q_ref.shape[0]) for h in range(GLA_HEADS)]

    def chunk(c, carry):
        rows = pl.ds(pl.multiple_of(c * CHUNK, CHUNK), CHUNK)
        qt, kt, kl, dec = {}, {}, {}, {}
        for n in range(q_ref.shape[0]):
            x = _dot_split2(_split_bf16(s_ref[n, rows, :], 2), wg_parts) + bg_ref[...]
            g = _log_sigmoid(x) * (1.0 / GLA_TAU)
            b = _dot_exact_by_f32(tri, g)
            b_last = b[CHUNK - 1:CHUNK, :]
            q = q_ref[n, rows, :].astype(F32)
            k = k_ref[n, rows, :].astype(F32)
            qt[n] = (q * scale * jnp.exp(b)).astype(BF16)
            kt[n] = (k * jnp.exp(-b)).astype(BF16)
            kl[n] = (k * jnp.exp(b_last - b)).astype(BF16)
            dec[n] = jnp.exp(b_last)
        att = {(n, h): jnp.where(causal, _dot_nt(qt[n][:, ks[h]], kt[n][:, ks[h]]), 0.0).astype(BF16)
               for n, h in pairs}
        st = {(n, h): st_ref[n, h] for n, h in pairs}
        vh = {(n, h): v_ref[n, rows, vs[h]] for n, h in pairs}
        o = {(n, h): _dot(att[n, h], vh[n, h]) + _dot_nt(qt[n][:, ks[h]], st[n, h].astype(BF16))
             for n, h in pairs}
        for n, h in pairs:
            st_ref[n, h] = st[n, h] * dec[n][:, ks[h]] + _dot_tn(vh[n, h], kl[n][:, ks[h]])
        for n, h in pairs:
            gate = r_ref[n, rows, vs[h]].astype(F32)
            o_ref[n, rows, vs[h]] = _head_norm_gate(o[n, h], ng_ref[...], gate).astype(o_ref.dtype)
        return carry

    lax.fori_loop(0, n_chunks, chunk, 0, unroll=2)


def _gla(u_gla, u_small, w_gate_pad, b_gate, norm_g):
    bsz, seq, _ = u_gla.shape
    t = _pick(seq, (768, 384, 256, 128, 64))
    col = lambda off, w: pl.BlockSpec((bsz, t, w), lambda i: (0, i, off // w))
    full = lambda a: pl.BlockSpec(a.shape, lambda i: (0,) * a.ndim)
    return pl.pallas_call(
        functools.partial(_gla_kernel, n_chunks=t // CHUNK),
        grid=(seq // t,),
        in_specs=[col(OFF_GQ, GLA_QK), col(OFF_GK, GLA_QK), col(OFF_GV, GLA_W), col(OFF_GR, GLA_W),
                  pl.BlockSpec((bsz, t, LANE), lambda i: (0, i, 0)),
                  full(w_gate_pad), full(b_gate), full(norm_g)],
        out_specs=pl.BlockSpec((bsz, t, GLA_W), lambda i: (0, i, 0)),
        out_shape=jax.ShapeDtypeStruct((bsz, seq, GLA_W), BF16),
        scratch_shapes=[pltpu.VMEM((bsz, GLA_HEADS, GLA_DV, GLA_DK), F32)],
        compiler_params=_params("arbitrary"),
        name="gla",
    )(u_gla, u_gla, u_gla, u_gla, u_small, w_gate_pad, b_gate, norm_g)


GDN_PREP_T = 128
HALO = 16


def _gdn_prep_kernel(q_ref, k_ref, v_ref, qh_ref, kh_ref, vh_ref, s_ref, cw_ref, al_ref, dt_ref,
                     eb_ref, eg_ref, qg_ref, kd_ref, u_ref, w_ref, aqk_ref, gl_ref,
                     qs_ref, ks_ref, vs_ref):
    first = pl.program_id(1) == 0
    t = q_ref.shape[0]

    out_r = lax.broadcasted_iota(jnp.int32, ((CONV_K - 1) * t, HALO + t), 0)
    in_r = lax.broadcasted_iota(jnp.int32, ((CONV_K - 1) * t, HALO + t), 1)
    t_log2 = t.bit_length() - 1
    assert 1 << t_log2 == t
    shift_op = jnp.where(in_r == HALO + (out_r & (t - 1)) - ((out_r >> t_log2) + 1),
                         1.0, 0.0).astype(BF16)

    def conv_silu(x_ref, h_ref, w):
        x = x_ref[...]
        hist = jnp.where(first, jnp.zeros_like(h_ref[...]), h_ref[...])
        taps = _dot(shift_op, jnp.concatenate([hist, x], axis=0))
        y = x.astype(F32) * w[CONV_K - 1:CONV_K, :]
        for j in range(1, CONV_K):
            y = y + taps[(j - 1) * t:j * t, :] * w[CONV_K - 1 - j:CONV_K - j, :]
        return y * _sigmoid(y)

    qc = conv_silu(q_ref, qh_ref, cw_ref[:, 0:GDN_QK])
    kc = conv_silu(k_ref, kh_ref, cw_ref[:, GDN_QK:2 * GDN_QK])
    vs_ref[...] = conv_silu(v_ref, vh_ref, cw_ref[:, 2 * GDN_QK:2 * GDN_QK + GDN_W])
    for h in range(GDN_HEADS):
        sl = slice(h * GDN_DK, (h + 1) * GDN_DK)
        qh = qc[:, sl]
        kh = kc[:, sl]
        qs_ref[:, sl] = qh * (lax.rsqrt(jnp.sum(qh * qh, axis=-1, keepdims=True) + EPS)
                              * GDN_DK ** -0.5)
        ks_ref[:, sl] = kh * lax.rsqrt(jnp.sum(kh * kh, axis=-1, keepdims=True) + EPS)

    s = s_ref[...]
    beta = _sigmoid(s)
    g = -jnp.exp(al_ref[...]) * _softplus(s + dt_ref[...])

    row = lax.broadcasted_iota(jnp.int32, (CHUNK, CHUNK), 0)
    col = lax.broadcasted_iota(jnp.int32, (CHUNK, CHUNK), 1)
    causal = row >= col
    strict = row > col
    tri = jnp.where(causal, 1.0, 0.0).astype(BF16)
    zeros_half = jnp.zeros((CHUNK, GDN_DV - CHUNK), F32)

    pairs = [(c, h) for c in range(t // CHUNK) for h in range(GDN_HEADS)]
    a_neg, rhs = {}, {}
    for c in range(t // CHUNK):
        rows = slice(c * CHUNK, (c + 1) * CHUNK)
        gcum = _dot_exact_by_f32(tri, g[rows])
        gcum_t = gcum.T
        gb = _dot_f32_by_exact(gcum, eg_ref[...])
        bb = _dot_f32_by_exact(beta[rows], eb_ref[...])
        g_last = gb[CHUNK - 1:CHUNK, :]
        eg = jnp.exp(gb)
        q = qs_ref[rows, :]
        k = ks_ref[rows, :]
        kb = k * bb
        qg_ref[rows, :] = (q * eg).astype(BF16)
        kd_ref[rows, :] = (k * jnp.exp(g_last - gb)).astype(BF16)
        gl_ref[c * SUBLANE:(c + 1) * SUBLANE, :] = jnp.broadcast_to(jnp.exp(g_last), (SUBLANE, GDN_W))
        rhs_v = vs_ref[rows, :] * bb
        rhs_k = kb * eg
        for h in range(GDN_HEADS):
            sl = slice(h * GDN_DK, (h + 1) * GDN_DK)
            lhs = jnp.concatenate([kb[:, sl], q[:, sl]], axis=0).astype(BF16)
            pr = _dot_nt(lhs, k[:, sl].astype(BF16))
            g_t = gb[:, h * GDN_DK:h * GDN_DK + CHUNK]
            g_s = gcum_t[S_A + h:S_A + h + 1, :]
            dec = jnp.exp(jnp.where(causal, g_t - g_s, -jnp.inf))
            a_neg[c, h] = jnp.where(strict, -pr[:CHUNK] * dec, 0.0)
            aqk_ref[rows, sl] = jnp.concatenate([pr[CHUNK:] * dec, zeros_half], axis=1).astype(BF16)
            rhs[c, h] = jnp.concatenate([rhs_v[:, sl], rhs_k[:, sl]], axis=1)
    toff = dict(a_neg)
    p = dict(a_neg)
    for _ in range(5):
        for key in pairs:
            pb = p[key].astype(BF16)
            p[key] = _dot(pb, pb)
        for key in pairs:
            toff[key] = toff[key] + p[key] + _dot(toff[key].astype(BF16), p[key].astype(BF16))
    for c, h in pairs:
        rows = slice(c * CHUNK, (c + 1) * CHUNK)
        sl = slice(h * GDN_DK, (h + 1) * GDN_DK)
        sol = rhs[c, h] + _dot(toff[c, h].astype(BF16), rhs[c, h].astype(BF16))
        u_ref[rows, sl] = sol[:, :GDN_DV]
        w_ref[rows, sl] = sol[:, GDN_DV:].astype(BF16)


def _gdn_scan_kernel(qg_ref, kd_ref, u_ref, w_ref, aqk_ref, gl_ref, z_ref, ng_ref, o_ref, st_ref,
                     *, n_chunks):
    @pl.when(pl.program_id(0) == 0)
    def _():
        st_ref[...] = jnp.zeros_like(st_ref)

    sls = [slice(h * GDN_DK, (h + 1) * GDN_DK) for h in range(GDN_HEADS)]
    pairs = [(n, h) for n in range(qg_ref.shape[0]) for h in range(GDN_HEADS)]

    def chunk(c, carry):
        rows = pl.ds(pl.multiple_of(c * CHUNK, CHUNK), CHUNK)
        grow = pl.ds(pl.multiple_of(c * SUBLANE, SUBLANE), SUBLANE)
        st = {(n, h): st_ref[n, h] for n, h in pairs}
        ws = {(n, h): _dot(jnp.concatenate([w_ref[n, rows, sls[h]], qg_ref[n, rows, sls[h]]], axis=0),
                           st[n, h].astype(BF16)) for n, h in pairs}
        vn = {(n, h): (u_ref[n, rows, sls[h]] - ws[n, h][:CHUNK]).astype(BF16) for n, h in pairs}
        o = {(n, h): ws[n, h][CHUNK:] + _dot(aqk_ref[n, rows, h * GDN_DK:h * GDN_DK + CHUNK], vn[n, h])
             for n, h in pairs}
        for n, h in pairs:
            st_ref[n, h] = (st[n, h] * gl_ref[n, grow, sls[h]][0:1, :]
                            + _dot_tn(kd_ref[n, rows, sls[h]], vn[n, h]))
        for n, h in pairs:
            gate = z_ref[n, rows, sls[h]].astype(F32)
            o_ref[n, rows, sls[h]] = _head_norm_gate(o[n, h], ng_ref[...], gate).astype(o_ref.dtype)
        return carry

    lax.fori_loop(0, n_chunks, chunk, 0)


def _gdn(u_main, u_small, conv_w, a_log_row, dt_row, e_beta, e_g, norm_g):
    bsz, seq, _ = u_main.shape
    t = GDN_PREP_T
    assert seq % t == 0 and t % HALO == 0
    blk = lambda off: pl.BlockSpec((None, t, GDN_W), lambda b, i: (b, i, off // GDN_W))
    halo = lambda off: pl.BlockSpec(
        (None, HALO, GDN_W), lambda b, i: (b, jnp.maximum(i * (t // HALO) - 1, 0), off // GDN_W))
    full = lambda a: pl.BlockSpec(a.shape, lambda b, i: (0,) * a.ndim)
    out = lambda: pl.BlockSpec((None, t, GDN_W), lambda b, i: (b, i, 0))
    gl_rows = t // CHUNK * SUBLANE
    sds = lambda dt: jax.ShapeDtypeStruct((bsz, seq, GDN_W), dt)
    qg, kd, u, w, aqk, gl = pl.pallas_call(
        _gdn_prep_kernel,
        grid=(bsz, seq // t),
        in_specs=[blk(OFF_DQ), blk(OFF_DK), blk(OFF_DV), halo(OFF_DQ), halo(OFF_DK), halo(OFF_DV),
                  pl.BlockSpec((None, t, LANE), lambda b, i: (b, i, 0)),
                  full(conv_w), full(a_log_row), full(dt_row), full(e_beta), full(e_g)],
        out_specs=[out(), out(), out(), out(), out(),
                   pl.BlockSpec((None, gl_rows, GDN_W), lambda b, i: (b, i, 0))],
        out_shape=[sds(BF16), sds(BF16), sds(F32), sds(BF16), sds(BF16),
                   jax.ShapeDtypeStruct((bsz, seq // CHUNK * SUBLANE, GDN_W), F32)],
        scratch_shapes=[pltpu.VMEM((t, GDN_W), F32)] * 3,
        compiler_params=_params("parallel", "parallel"),
        name="gdn_prep",
    )(u_main, u_main, u_main, u_main, u_main, u_main, u_small, conv_w, a_log_row, dt_row, e_beta, e_g)

    ts = _pick(seq, (256, 128, 64))
    blk = lambda: pl.BlockSpec((bsz, ts, GDN_W), lambda i: (0, i, 0))
    return pl.pallas_call(
        functools.partial(_gdn_scan_kernel, n_chunks=ts // CHUNK),
        grid=(seq // ts,),
        in_specs=[blk(), blk(), blk(), blk(), blk(),
                  pl.BlockSpec((bsz, ts // CHUNK * SUBLANE, GDN_W), lambda i: (0, i, 0)),
                  pl.BlockSpec((bsz, ts, GDN_W), lambda i: (0, i, OFF_DZ // GDN_W)),
                  pl.BlockSpec(norm_g.shape, lambda i: (0, 0))],
        out_specs=blk(),
        out_shape=jax.ShapeDtypeStruct((bsz, seq, GDN_W), BF16),
        scratch_shapes=[pltpu.VMEM((bsz, GDN_HEADS, GDN_DK, GDN_DV), F32)],
        compiler_params=_params("arbitrary"),
        name="gdn_scan",
    )(qg, kd, u, w, aqk, gl, u_main, norm_g)


SB_RB = 128
SB_LOOK = 2
SB_QT = 2 * SB_RB
SB_EXP_ZERO = 104.0


def _sb_kernel(q_ref, k_ref, v_ref, g_ref, ng_ref, o_ref, acc_ref, run_ref, z_buf, hl_buf, a_buf,
               *, n_invalid):
    seq = q_ref.shape[0]
    rb = SB_RB
    scale = SB_D ** -0.5
    jr = lax.broadcasted_iota(jnp.int32, (2 * rb, 2 * rb), 0)
    sc = lax.broadcasted_iota(jnp.int32, (2 * rb, 2 * rb), 1)
    rr = jnp.where((sc >= rb) | ((jr & (rb - 1)) >= sc), 1.0, 0.0).astype(BF16)
    row_i = lax.broadcasted_iota(jnp.int32, (rb, rb), 0)
    col_i = lax.broadcasted_iota(jnp.int32, (rb, rb), 1)
    below_diag = col_i < row_i

    def softplus(z):
        return jnp.maximum(z, 0.0) + jnp.log(1.0 + jnp.exp2(jnp.abs(z) * -LOG2E))

    def weights(z, sp, run):
        hi = sp.astype(BF16)
        lo = (sp - hi.astype(F32)).astype(BF16)
        cs = _dot(jnp.concatenate([hi, lo], axis=1), rr)
        return jnp.exp(z - cs[:, :rb] - run).astype(BF16), cs[:, rb:]

    def mask(z, sp, vis):
        return jnp.where(vis, z, -jnp.inf), jnp.where(vis, sp, 0.0)

    def finish(rows, acc):
        gate = g_ref[rows, :].astype(F32)
        o_ref[rows, :] = _head_norm_gate(acc, ng_ref[...], gate).astype(o_ref.dtype)

    def row_block(zs, key0, check_valid):
        run = jnp.zeros((rb, rb), F32)
        parts = [None] * len(zs)
        for c in reversed(range(len(zs))):
            z, sp = zs[c], softplus(zs[c])
            vis = below_diag if c == len(zs) - 1 else None
            if check_valid:
                ok = (key0 + c * rb + col_i) >= n_invalid
                vis = ok if vis is None else (vis & ok)
            if vis is not None:
                z, sp = mask(z, sp, vis)
            parts[c], tot = weights(z, sp, run)
            run = run + tot
        return jnp.concatenate(parts, axis=1), run

    def qtile(r0, looks, check_valid):
        k0 = r0 - looks[0] * rb
        nk = looks[0] + 2
        zz = _dot_nt(q_ref[pl.ds(r0, SB_QT), :], k_ref[pl.ds(k0, nk * rb), :]) * scale
        run_min = None
        for r in range(2):
            c0 = r - looks[r] + looks[0]
            zs = [zz[r * rb:(r + 1) * rb, c * rb:(c + 1) * rb] for c in range(c0, c0 + looks[r] + 1)]
            a, run = row_block(zs, k0 + c0 * rb, check_valid)
            acc = _dot(a, v_ref[pl.ds(k0 + c0 * rb, (looks[r] + 1) * rb), :])
            rows = pl.ds(r0 + r * rb, rb)
            acc_ref[rows, :] = acc
            run_ref[rows, :] = run
            finish(rows, acc)
            run_min = run if run_min is None else jnp.minimum(run_min, run)
        return run_min

    first_open = n_invalid // rb + SB_LOOK + 1
    assert first_open % 2 == 0 and first_open * rb <= seq and (first_open - SB_LOOK) * rb >= n_invalid
    qtile(0, (0, 1), True)
    for qi in range(1, first_open // 2):
        qtile(qi * SB_QT, (SB_LOOK, SB_LOOK), True)

    n_sub = SB_LOOK + 1
    t0 = first_open // 2
    n_tiles = seq // SB_QT - t0

    def tile_row(t):
        r0 = (t0 + t) * SB_QT
        return r0 if isinstance(t, int) else pl.multiple_of(r0, SB_QT)

    def scores(t):
        r0 = tile_row(t)
        k0 = r0 - SB_LOOK * rb
        zz = _dot_nt(q_ref[pl.ds(r0, SB_QT), :], k_ref[pl.ds(k0, (SB_LOOK + 2) * rb), :]) * scale
        for r in range(2):
            for c in range(n_sub):
                z = zz[r * rb:(r + 1) * rb, (r + c) * rb:(r + c + 1) * rb]
                sp = softplus(z)
                if c == n_sub - 1:
                    z, sp = mask(z, sp, below_diag)
                hi = sp.astype(BF16)
                lo = (sp - hi.astype(F32)).astype(BF16)
                z_buf[r, c] = z
                hl_buf[r, c] = jnp.concatenate([hi, lo], axis=1)

    def weights_of(t):
        r0 = tile_row(t)
        run_min = None
        for r in range(2):
            run = jnp.zeros((rb, rb), F32)
            for c in reversed(range(n_sub)):
                cs = _dot(hl_buf[r, c], rr)
                a_buf[r, :, c * rb:(c + 1) * rb] = jnp.exp(z_buf[r, c] - cs[:, :rb] - run).astype(BF16)
                run = run + cs[:, rb:]
            run_ref[pl.ds(r0 + r * rb, rb), :] = run
            run_min = run if run_min is None else jnp.minimum(run_min, run)
        return run_min

    def values(t):
        r0 = tile_row(t)
        k0 = r0 - SB_LOOK * rb
        for r in range(2):
            acc = _dot(a_buf[r], v_ref[pl.ds(k0 + r * rb, n_sub * rb), :])
            rows = pl.ds(r0 + r * rb, rb)
            acc_ref[rows, :] = acc
            finish(rows, acc)

    def step(s, run_min, stages):
        if stages[2]:
            values(s - 2)
        if stages[1]:
            run_min = jnp.minimum(run_min, weights_of(s - 1))
        if stages[0]:
            scores(s)
        return run_min

    run_min = jnp.full((rb, rb), 2 * SB_EXP_ZERO, F32)
    for s in range(n_tiles + 2):
        if s == 2 and n_tiles > 2:
            run_min = lax.fori_loop(2, n_tiles, lambda t, m: step(t, m, (True, True, True)), run_min,
                                    unroll=2)
        if 2 <= s < n_tiles:
            continue
        run_min = step(s, run_min, [0 <= s - k < n_tiles for k in (0, 1, 2)])

    @pl.when(jnp.min(run_min) < SB_EXP_ZERO)
    def _():
        def second_pass(b, carry):
            rows = pl.ds(pl.multiple_of(b * rb, rb), rb)

            def unfinished(state):
                j, lowest = state
                return (j >= 0) & (lowest < SB_EXP_ZERO)

            def sub_block(state):
                j, _ = state
                keys = pl.ds(pl.multiple_of(j * rb, rb), rb)
                z = _dot_nt(q_ref[rows, :], k_ref[keys, :]) * scale
                z, sp = mask(z, softplus(z), (j * rb + col_i) >= n_invalid)
                run = run_ref[rows, :]
                a, tot = weights(z, sp, run)
                acc_ref[rows, :] += _dot(a, v_ref[keys, :])
                run_ref[rows, :] = run + tot
                return j - 1, jnp.min(run + tot)

            lax.while_loop(unfinished, sub_block, (b - SB_LOOK - 1, jnp.min(run_ref[rows, :])))
            finish(rows, acc_ref[rows, :])
            return carry

        lax.fori_loop(first_open, seq // rb, second_pass, 0)


def _sb(u_main, norm_g, n_invalid):
    bsz, seq, _ = u_main.shape
    assert seq % SB_QT == 0 and SB_D == SB_RB
    blk = lambda off: pl.BlockSpec((None, seq, SB_D), lambda b, h: (b, 0, off // SB_D + h))
    return pl.pallas_call(
        functools.partial(_sb_kernel, n_invalid=n_invalid),
        grid=(bsz, SB_HEADS),
        in_specs=[blk(OFF_SQ), blk(OFF_SK), blk(OFF_SV), blk(OFF_SG),
                  pl.BlockSpec(norm_g.shape, lambda b, h: (0, 0))],
        out_specs=pl.BlockSpec((None, seq, SB_D), lambda b, h: (b, 0, h)),
        out_shape=jax.ShapeDtypeStruct((bsz, seq, SB_W), BF16),
        scratch_shapes=[pltpu.VMEM((seq, SB_D), F32), pltpu.VMEM((seq, SB_D), F32),
                        pltpu.VMEM((2, SB_LOOK + 1, SB_RB, SB_RB), F32),
                        pltpu.VMEM((2, SB_LOOK + 1, SB_RB, 2 * SB_RB), BF16),
                        pltpu.VMEM((2, SB_RB, (SB_LOOK + 1) * SB_RB), BF16)],
        compiler_params=_params("parallel", "parallel"),
        name="stick_breaking",
    )(u_main, u_main, u_main, u_main, norm_g)


def _place(vec, lane0):
    return jnp.zeros((1, LANE), F32).at[0, lane0:lane0 + vec.shape[0]].set(vec.astype(F32))


def _expand(lane0, heads, width):
    src = jnp.arange(LANE)[:, None]
    dst_head = jnp.arange(heads * width)[None, :] // width
    return (src == lane0 + dst_head).astype(BF16)


def _narrow_weights(w_t, layer):
    o = _IN_OFF
    base = layer * o[-1]
    return jnp.concatenate(
        [w_t[base + o[4]:base + o[5]], w_t[base + o[9]:base + o[11]],
         jnp.zeros((LANE - GLA_RANK - 2 * GDN_HEADS, w_t.shape[1]), w_t.dtype)], axis=0)


def _layer(h, xn, u_small, w_t, w_out, layer, gla_w_gate, gla_b_gate, gla_norm_g, gdn_conv_w,
           gdn_a_log, gdn_dt_bias, gdn_norm_g, sb_norm_g, bsz, seq):
    m = bsz * seq
    o = _IN_OFF
    base = layer * o[-1]

    def project(lo, hi, name):
        return _in_proj(xn, w_t, base + lo, hi - lo, BF16, name).reshape(bsz, seq, hi - lo)

    u_gla = project(o[0], o[4], "in_proj_gla")
    u_gdn = project(o[5], o[9], "in_proj_gdn")
    u_sb = project(o[11], o[15], "in_proj_sb")
    u_small = u_small.reshape(bsz, seq, LANE)

    w_gate_pad = jnp.zeros((LANE, GLA_QK), F32).at[S_LR:S_LR + GLA_RANK].set(gla_w_gate)
    o_gla = _gla(u_gla, u_small, w_gate_pad, gla_b_gate.reshape(1, GLA_QK),
                 gla_norm_g.reshape(1, GLA_DV))
    o_gdn = _gdn(u_gdn, u_small, gdn_conv_w, _place(gdn_a_log, S_A), _place(gdn_dt_bias, S_A),
                 _expand(S_B, GDN_HEADS, GDN_DK), _expand(S_A, GDN_HEADS, GDN_DK),
                 gdn_norm_g.reshape(1, GDN_DV))
    o_sb = _sb(u_sb, sb_norm_g.reshape(1, SB_D), PREFIX - N_META)

    return _out_proj(o_gla.reshape(m, GLA_W), o_gdn.reshape(m, GDN_W), o_sb.reshape(m, SB_W),
                     w_out, layer, h)


def kernel(x, meta, norm_g, w_in, gla_w_gate, gla_b_gate, gla_norm_g, gdn_conv_w, gdn_a_log,
           gdn_dt_bias, gdn_norm_g, sb_norm_g, w_out, final_g):
    bsz, n, d = x.shape
    prefix = jnp.concatenate([jnp.zeros((PREFIX - N_META, d), x.dtype), meta.astype(x.dtype)], axis=0)
    seq = PREFIX + n
    w_t = jnp.swapaxes(w_in, 1, 2).reshape(-1, d)
    h, xn, u_small = _embed_norm(x, prefix, norm_g[0], _narrow_weights(w_t, 0))
    h = h.reshape(bsz * seq, d)
    xn = xn.reshape(bsz * seq, d)
    for l in range(norm_g.shape[0]):
        if l > 0:
            xn, u_small = _rmsnorm_narrow(h, norm_g[l], _narrow_weights(w_t, l))
        h = _layer(h, xn, u_small, w_t, w_out, l, gla_w_gate[l], gla_b_gate[l], gla_norm_g[l], gdn_conv_w[l],
                   gdn_a_log[l], gdn_dt_bias[l], gdn_norm_g[l], sb_norm_g[l], bsz, seq)
    return _final_rmsnorm(h.reshape(bsz, seq, d), final_g, PREFIX)
```

```python
import functools

import jax
import jax.numpy as jnp
from jax import lax
from jax.experimental import pallas as pl
from jax.experimental.pallas import tpu as pltpu

F32 = jnp.float32
BF16 = jnp.bfloat16

N_META = 16
PREFIX = 256
CHUNK = 64
EPS = 1e-6
LOG2E = 1.4426950408889634

GLA_HEADS, GLA_DK, GLA_DV, GLA_RANK, GLA_TAU = 4, 128, 256, 16, 16.0
GDN_HEADS, GDN_DK, GDN_DV, CONV_K = 12, 128, 128, 4
SB_HEADS, SB_D = 12, 128

GLA_QK = GLA_HEADS * GLA_DK
GLA_W = GLA_HEADS * GLA_DV
GDN_QK = GDN_HEADS * GDN_DK
GDN_W = GDN_HEADS * GDN_DV
SB_W = SB_HEADS * SB_D

_IN_SPLITS = (GLA_QK, GLA_QK, GLA_W, GLA_W, GLA_RANK,
              GDN_QK, GDN_QK, GDN_W, GDN_W, GDN_HEADS, GDN_HEADS,
              SB_W, SB_W, SB_W, SB_W)
_IN_OFF = [0]
for _w in _IN_SPLITS:
    _IN_OFF.append(_IN_OFF[-1] + _w)

OFF_GQ, OFF_GK, OFF_GV, OFF_GR = 0, 512, 1024, 2048
OFF_DQ, OFF_DK, OFF_DV, OFF_DZ = 0, 1536, 3072, 4608
OFF_SQ, OFF_SK, OFF_SV, OFF_SG = 0, 1536, 3072, 4608
LANE = 128
SUBLANE = 8
S_LR, S_B, S_A = 0, GLA_RANK, GLA_RANK + GDN_HEADS

V7X_VMEM_LIMIT_BYTES = 56 * 1024 * 1024


def _pick(n, candidates):
    for c in candidates:
        if n % c == 0:
            return c
    raise ValueError(f"no block size in {candidates} divides {n}")


def _params(*sem):
    return pltpu.CompilerParams(dimension_semantics=sem, vmem_limit_bytes=V7X_VMEM_LIMIT_BYTES)


def _dot(a, b):
    return jnp.dot(a, b, preferred_element_type=F32)


def _dot_nt(a, b):
    return lax.dot_general(a, b, (((1,), (1,)), ((), ())), preferred_element_type=F32)


def _dot_tn(a, b):
    return lax.dot_general(a, b, (((0,), (0,)), ((), ())), preferred_element_type=F32)


def _split_bf16(x, n):
    parts = []
    for _ in range(n):
        p = x.astype(BF16)
        parts.append(p)
        x = x - p.astype(F32)
    return parts


def _dot_f32_by_exact(a, b_exact):
    hi, mid, lo = _split_bf16(a, 3)
    return _dot(hi, b_exact) + _dot(mid, b_exact) + _dot(lo, b_exact)


def _dot_exact_by_f32(a_exact, b):
    hi, mid, lo = _split_bf16(b, 3)
    return _dot(a_exact, hi) + _dot(a_exact, mid) + _dot(a_exact, lo)


def _dot_split2(a_parts, b_parts):
    return _dot(a_parts[0], b_parts[0]) + _dot(a_parts[0], b_parts[1]) + _dot(a_parts[1], b_parts[0])


def _exp_neg(x):
    return jnp.exp2(x * -LOG2E)


def _softplus(x):
    return jnp.maximum(x, 0.0) + jnp.log(1.0 + _exp_neg(jnp.abs(x)))


def _log_sigmoid(x):
    return jnp.minimum(x, 0.0) - jnp.log(1.0 + _exp_neg(jnp.abs(x)))


def _sigmoid(x):
    return 1.0 / (1.0 + _exp_neg(x))


def _head_norm_gate(o, norm_g, gate):
    ms = jnp.mean(o * o, axis=-1, keepdims=True)
    return o * lax.rsqrt(ms + EPS) * norm_g * (gate * _sigmoid(gate))


def _rmsnorm_kernel(x_ref, g_ref, o_ref):
    x = x_ref[...]
    ms = jnp.mean(x * x, axis=-1, keepdims=True)
    o_ref[...] = (x * lax.rsqrt(ms + EPS) * g_ref[...]).astype(o_ref.dtype)


def _norm_and_narrow(x, g_ref, ws_ref, o_ref, s_ref):
    ms = jnp.mean(x * x, axis=-1, keepdims=True)
    xn = (x * lax.rsqrt(ms + EPS) * g_ref[...]).astype(BF16)
    o_ref[...] = xn
    s_ref[...] = _dot_nt(xn, ws_ref[...].astype(BF16))


def _rmsnorm_narrow_kernel(x_ref, g_ref, ws_ref, o_ref, s_ref):
    _norm_and_narrow(x_ref[...], g_ref, ws_ref, o_ref, s_ref)


def _rmsnorm_narrow(x, g, w_small):
    m, d = x.shape
    tm = _pick(m, (512, 256, 128, 64, 8))
    return pl.pallas_call(
        _rmsnorm_narrow_kernel,
        grid=(m // tm,),
        in_specs=[pl.BlockSpec((tm, d), lambda i: (i, 0)),
                  pl.BlockSpec((1, d), lambda i: (0, 0)),
                  pl.BlockSpec(w_small.shape, lambda i: (0, 0))],
        out_specs=[pl.BlockSpec((tm, d), lambda i: (i, 0)),
                   pl.BlockSpec((tm, LANE), lambda i: (i, 0))],
        out_shape=[jax.ShapeDtypeStruct((m, d), BF16), jax.ShapeDtypeStruct((m, LANE), F32)],
        compiler_params=_params("parallel"),
        name="rmsnorm",
    )(x, g.reshape(1, d), w_small)


def _embed_norm_kernel(x_ref, p_ref, g_ref, ws_ref, h_ref, o_ref, s_ref):
    x = jnp.where(pl.program_id(1) == 0, p_ref[...], x_ref[...])
    h_ref[...] = x
    _norm_and_narrow(x, g_ref, ws_ref, o_ref, s_ref)


def _embed_norm(x, prefix, g, w_small):
    bsz, n, d = x.shape
    tm = prefix.shape[0]
    assert n % tm == 0
    blk = lambda w: pl.BlockSpec((None, tm, w), lambda b, i: (b, i, 0))
    return pl.pallas_call(
        _embed_norm_kernel,
        grid=(bsz, n // tm + 1),
        in_specs=[pl.BlockSpec((None, tm, d), lambda b, i: (b, jnp.maximum(i - 1, 0), 0)),
                  pl.BlockSpec((tm, d), lambda b, i: (0, 0)),
                  pl.BlockSpec((1, d), lambda b, i: (0, 0)),
                  pl.BlockSpec(w_small.shape, lambda b, i: (0, 0))],
        out_specs=[blk(d), blk(d), blk(LANE)],
        out_shape=[jax.ShapeDtypeStruct((bsz, n + tm, d), x.dtype),
                   jax.ShapeDtypeStruct((bsz, n + tm, d), BF16),
                   jax.ShapeDtypeStruct((bsz, n + tm, LANE), F32)],
        compiler_params=_params("parallel", "parallel"),
        name="embed_norm",
    )(x, prefix, g.reshape(1, d), w_small)


def _final_rmsnorm(h, g, n_skip):
    bsz, seq, d = h.shape
    tm = _pick(n_skip, (256, 128, 64, 8))
    assert (seq - n_skip) % tm == 0
    return pl.pallas_call(
        _rmsnorm_kernel,
        grid=(bsz, (seq - n_skip) // tm),
        in_specs=[pl.BlockSpec((None, tm, d), lambda b, i: (b, i + n_skip // tm, 0)),
                  pl.BlockSpec((1, d), lambda b, i: (0, 0))],
        out_specs=pl.BlockSpec((None, tm, d), lambda b, i: (b, i, 0)),
        out_shape=jax.ShapeDtypeStruct((bsz, seq - n_skip, d), h.dtype),
        compiler_params=_params("parallel", "parallel"),
        name="final_rmsnorm",
    )(h, g.reshape(1, d))


PROJ_TN = 512
CAST_ROWS = 512


def _cast_weight_tile(dst_ref, src_ref):
    step = min(CAST_ROWS, dst_ref.shape[0])

    def chunk(r, carry):
        rows = pl.ds(pl.multiple_of(r * step, step), step)
        dst_ref[rows, :] = src_ref[rows, :].astype(BF16)
        return carry

    lax.fori_loop(0, dst_ref.shape[0] // step, chunk, 0)


def _in_proj_kernel(x_ref, wt_ref, o_ref, w_ref):
    @pl.when(pl.program_id(1) == 0)
    def _():
        for c in range(wt_ref.shape[0] // LANE):
            cols = slice(c * LANE, (c + 1) * LANE)
            w_ref[:, cols] = wt_ref[cols, :].T.astype(BF16)

    o_ref[...] = _dot(x_ref[...], w_ref[...]).astype(o_ref.dtype)


def _in_proj(xn, w_t, row0, width, name):
    m, k = xn.shape
    tn = min(PROJ_TN, width)
    assert width % tn == 0 and row0 % SUBLANE == 0 and tn % LANE == 0
    tm = _pick(m, (1536, 1056, 768, 512, 256, 128))
    return pl.pallas_call(
        _in_proj_kernel,
        grid=(width // tn, m // tm),
        in_specs=[pl.BlockSpec((tm, k), lambda j, i: (i, 0)),
                  pl.BlockSpec((pl.Element(tn), pl.Element(k)),
                               lambda j, i: (pl.multiple_of(row0 + j * tn, SUBLANE), 0))],
        out_specs=pl.BlockSpec((tm, tn), lambda j, i: (i, j)),
        out_shape=jax.ShapeDtypeStruct((m, width), BF16),
        scratch_shapes=[pltpu.VMEM((k, tn), BF16)],
        compiler_params=_params("arbitrary", "arbitrary"),
        name=name,
    )(xn, w_t)


def _out_proj_kernel(a1_ref, a2_ref, a3_ref, wf_ref, h_ref, o_ref, w_ref):
    @pl.when(pl.program_id(1) == 0)
    def _():
        _cast_weight_tile(w_ref, wf_ref)

    k1, k2 = a1_ref.shape[1], a2_ref.shape[1]
    y = _dot(a1_ref[...], w_ref[0:k1, :])
    y = y + _dot(a2_ref[...], w_ref[k1:k1 + k2, :])
    y = y + _dot(a3_ref[...], w_ref[k1 + k2:, :])
    o_ref[...] = h_ref[...] + y


def _out_proj(a1, a2, a3, w_out, layer, h):
    m, d = h.shape
    k = w_out.shape[1]
    tn = PROJ_TN
    assert a1.shape[1] + a2.shape[1] + a3.shape[1] == k and d % tn == 0 and k % CAST_ROWS == 0
    tm = _pick(m, (1056, 768, 512, 256, 128))
    act = lambda a: pl.BlockSpec((tm, a.shape[1]), lambda j, i: (i, 0))
    return pl.pallas_call(
        _out_proj_kernel,
        grid=(d // tn, m // tm),
        in_specs=[act(a1), act(a2), act(a3),
                  pl.BlockSpec((None, k, tn), lambda j, i: (layer, 0, j)),
                  pl.BlockSpec((tm, tn), lambda j, i: (i, j))],
        out_specs=pl.BlockSpec((tm, tn), lambda j, i: (i, j)),
        out_shape=jax.ShapeDtypeStruct((m, d), F32),
        scratch_shapes=[pltpu.VMEM((k, tn), BF16)],
        compiler_params=_params("arbitrary", "arbitrary"),
        name="out_proj",
    )(a1, a2, a3, w_out, h)


def _gla_kernel(q_ref, k_ref, v_ref, r_ref, s_ref, wg_ref, bg_ref, ng_ref, o_ref, st_ref,
                *, n_chunks):
    @pl.when(pl.program_id(0) == 0)
    def _():
        st_ref[...] = jnp.zeros_like(st_ref)

    row = lax.broadcasted_iota(jnp.int32, (CHUNK, CHUNK), 0)
    col = lax.broadcasted_iota(jnp.int32, (CHUNK, CHUNK), 1)
    causal = row >= col
    tri = jnp.where(causal, 1.0, 0.0).astype(BF16)
    scale = GLA_DK ** -0.5
    wg_parts = _split_bf16(wg_ref[...], 2)
    ks = [slice(h * GLA_DK, (h + 1) * GLA_DK) for h in range(GLA_HEADS)]
    vs = [slice(h * GLA_DV, (h + 1) * GLA_DV) for h in range(GLA_HEADS)]
    pairs = [(n, h) for n in range(q_ref.shape[0]) for h in range(GLA_HEADS)]

    def chunk(c, carry):
        rows = pl.ds(pl.multiple_of(c * CHUNK, CHUNK), CHUNK)
        qt, kt, kl, dec = {}, {}, {}, {}
        for n in range(q_ref.shape[0]):
            x = _dot_split2(_split_bf16(s_ref[n, rows, :], 2), wg_parts) + bg_ref[...]
            g = _log_sigmoid(x) * (1.0 / GLA_TAU)
            b = _dot_exact_by_f32(tri, g)
            b_last = b[CHUNK - 1:CHUNK, :]
            q = q_ref[n, rows, :].astype(F32)
            k = k_ref[n, rows, :].astype(F32)
            qt[n] = (q * scale * jnp.exp(b)).astype(BF16)
            kt[n] = (k * jnp.exp(-b)).astype(BF16)
            kl[n] = (k * jnp.exp(b_last - b)).astype(BF16)
            dec[n] = jnp.exp(b_last)
        att = {(n, h): jnp.where(causal, _dot_nt(qt[n][:, ks[h]], kt[n][:, ks[h]]), 0.0).astype(BF16)
               for n, h in pairs}
        st = {(n, h): st_ref[n, h] for n, h in pairs}
        vh = {(n, h): v_ref[n, rows, vs[h]] for n, h in pairs}
        o = {(n, h): _dot(att[n, h], vh[n, h]) + _dot_nt(qt[n][:, ks[h]], st[n, h].astype(BF16))
             for n, h in pairs}
        for n, h in pairs:
            st_ref[n, h] = st[n, h] * dec[n][:, ks[h]] + _dot_tn(vh[n, h], kl[n][:, ks[h]])
        for n, h in pairs:
            gate = r_ref[n, rows, vs[h]].astype(F32)
            o_ref[n, rows, vs[h]] = _head_norm_gate(o[n, h], ng_ref[...], gate).astype(o_ref.dtype)
        return carry

    lax.fori_loop(0, n_chunks, chunk, 0, unroll=2)


def _gla(u_gla, u_small, w_gate_pad, b_gate, norm_g):
    bsz, seq, _ = u_gla.shape
    t = _pick(seq, (768, 384, 256, 128, 64))
    col = lambda off, w: pl.BlockSpec((bsz, t, w), lambda i: (0, i, off // w))
    full = lambda a: pl.BlockSpec(a.shape, lambda i: (0,) * a.ndim)
    return pl.pallas_call(
        functools.partial(_gla_kernel, n_chunks=t // CHUNK),
        grid=(seq // t,),
        in_specs=[col(OFF_GQ, GLA_QK), col(OFF_GK, GLA_QK), col(OFF_GV, GLA_W), col(OFF_GR, GLA_W),
                  pl.BlockSpec((bsz, t, LANE), lambda i: (0, i, 0)),
                  full(w_gate_pad), full(b_gate), full(norm_g)],
        out_specs=pl.BlockSpec((bsz, t, GLA_W), lambda i: (0, i, 0)),
        out_shape=jax.ShapeDtypeStruct((bsz, seq, GLA_W), BF16),
        scratch_shapes=[pltpu.VMEM((bsz, GLA_HEADS, GLA_DV, GLA_DK), F32)],
        compiler_params=_params("arbitrary"),
        name="gla",
    )(u_gla, u_gla, u_gla, u_gla, u_small, w_gate_pad, b_gate, norm_g)


GDN_PREP_T = 128
HALO = 16


def _gdn_prep_kernel(q_ref, k_ref, v_ref, qh_ref, kh_ref, vh_ref, s_ref, cw_ref, al_ref, dt_ref,
                     eb_ref, eg_ref, qg_ref, kd_ref, u_ref, w_ref, aqk_ref, gl_ref,
                     qs_ref, ks_ref, vs_ref):
    first = pl.program_id(1) == 0
    t = q_ref.shape[0]

    out_r = lax.broadcasted_iota(jnp.int32, ((CONV_K - 1) * t, HALO + t), 0)
    in_r = lax.broadcasted_iota(jnp.int32, ((CONV_K - 1) * t, HALO + t), 1)
    t_log2 = t.bit_length() - 1
    assert 1 << t_log2 == t
    shift_op = jnp.where(in_r == HALO + (out_r & (t - 1)) - ((out_r >> t_log2) + 1),
                         1.0, 0.0).astype(BF16)

    def conv_silu(x_ref, h_ref, w):
        x = x_ref[...]
        hist = jnp.where(first, jnp.zeros_like(h_ref[...]), h_ref[...])
        taps = _dot(shift_op, jnp.concatenate([hist, x], axis=0))
        y = x.astype(F32) * w[CONV_K - 1:CONV_K, :]
        for j in range(1, CONV_K):
            y = y + taps[(j - 1) * t:j * t, :] * w[CONV_K - 1 - j:CONV_K - j, :]
        return y * _sigmoid(y)

    qc = conv_silu(q_ref, qh_ref, cw_ref[:, 0:GDN_QK])
    kc = conv_silu(k_ref, kh_ref, cw_ref[:, GDN_QK:2 * GDN_QK])
    vs_ref[...] = conv_silu(v_ref, vh_ref, cw_ref[:, 2 * GDN_QK:2 * GDN_QK + GDN_W])
    for h in range(GDN_HEADS):
        sl = slice(h * GDN_DK, (h + 1) * GDN_DK)
        qh = qc[:, sl]
        kh = kc[:, sl]
        qs_ref[:, sl] = qh * (lax.rsqrt(jnp.sum(qh * qh, axis=-1, keepdims=True) + EPS)
                              * GDN_DK ** -0.5)
        ks_ref[:, sl] = kh * lax.rsqrt(jnp.sum(kh * kh, axis=-1, keepdims=True) + EPS)

    s = s_ref[...]
    beta = _sigmoid(s)
    g = -jnp.exp(al_ref[...]) * _softplus(s + dt_ref[...])

    row = lax.broadcasted_iota(jnp.int32, (CHUNK, CHUNK), 0)
    col = lax.broadcasted_iota(jnp.int32, (CHUNK, CHUNK), 1)
    causal = row >= col
    strict = row > col
    tri = jnp.where(causal, 1.0, 0.0).astype(BF16)
    zeros_half = jnp.zeros((CHUNK, GDN_DV - CHUNK), F32)

    pairs = [(c, h) for c in range(t // CHUNK) for h in range(GDN_HEADS)]
    a_neg, rhs = {}, {}
    for c in range(t // CHUNK):
        rows = slice(c * CHUNK, (c + 1) * CHUNK)
        gcum = _dot_exact_by_f32(tri, g[rows])
        gcum_t = gcum.T
        gb = _dot_f32_by_exact(gcum, eg_ref[...])
        bb = _dot_f32_by_exact(beta[rows], eb_ref[...])
        g_last = gb[CHUNK - 1:CHUNK, :]
        eg = jnp.exp(gb)
        q = qs_ref[rows, :]
        k = ks_ref[rows, :]
        kb = k * bb
        qg_ref[rows, :] = (q * eg).astype(BF16)
        kd_ref[rows, :] = (k * jnp.exp(g_last - gb)).astype(BF16)
        gl_ref[c * SUBLANE:(c + 1) * SUBLANE, :] = jnp.broadcast_to(jnp.exp(g_last), (SUBLANE, GDN_W))
        rhs_v = vs_ref[rows, :] * bb
        rhs_k = kb * eg
        for h in range(GDN_HEADS):
            sl = slice(h * GDN_DK, (h + 1) * GDN_DK)
            lhs = jnp.concatenate([kb[:, sl], q[:, sl]], axis=0).astype(BF16)
            pr = _dot_nt(lhs, k[:, sl].astype(BF16))
            g_t = gb[:, h * GDN_DK:h * GDN_DK + CHUNK]
            g_s = gcum_t[S_A + h:S_A + h + 1, :]
            dec = jnp.exp(jnp.where(causal, g_t - g_s, -jnp.inf))
            a_neg[c, h] = jnp.where(strict, -pr[:CHUNK] * dec, 0.0)
            aqk_ref[rows, sl] = jnp.concatenate([pr[CHUNK:] * dec, zeros_half], axis=1).astype(BF16)
            rhs[c, h] = jnp.concatenate([rhs_v[:, sl], rhs_k[:, sl]], axis=1)
    toff = dict(a_neg)
    p = dict(a_neg)
    for _ in range(5):
        for key in pairs:
            pb = p[key].astype(BF16)
            p[key] = _dot(pb, pb)
        for key in pairs:
            toff[key] = toff[key] + p[key] + _dot(toff[key].astype(BF16), p[key].astype(BF16))
    for c, h in pairs:
        rows = slice(c * CHUNK, (c + 1) * CHUNK)
        sl = slice(h * GDN_DK, (h + 1) * GDN_DK)
        sol = rhs[c, h] + _dot(toff[c, h].astype(BF16), rhs[c, h].astype(BF16))
        u_ref[rows, sl] = sol[:, :GDN_DV]
        w_ref[rows, sl] = sol[:, GDN_DV:].astype(BF16)


def _gdn_scan_kernel(qg_ref, kd_ref, u_ref, w_ref, aqk_ref, gl_ref, z_ref, ng_ref, o_ref, st_ref,
                     *, n_chunks):
    @pl.when(pl.program_id(0) == 0)
    def _():
        st_ref[...] = jnp.zeros_like(st_ref)

    sls = [slice(h * GDN_DK, (h + 1) * GDN_DK) for h in range(GDN_HEADS)]
    pairs = [(n, h) for n in range(qg_ref.shape[0]) for h in range(GDN_HEADS)]

    def chunk(c, carry):
        rows = pl.ds(pl.multiple_of(c * CHUNK, CHUNK), CHUNK)
        grow = pl.ds(pl.multiple_of(c * SUBLANE, SUBLANE), SUBLANE)
        st = {(n, h): st_ref[n, h] for n, h in pairs}
        ws = {(n, h): _dot(jnp.concatenate([w_ref[n, rows, sls[h]], qg_ref[n, rows, sls[h]]], axis=0),
                           st[n, h].astype(BF16)) for n, h in pairs}
        vn = {(n, h): (u_ref[n, rows, sls[h]] - ws[n, h][:CHUNK]).astype(BF16) for n, h in pairs}
        o = {(n, h): ws[n, h][CHUNK:] + _dot(aqk_ref[n, rows, h * GDN_DK:h * GDN_DK + CHUNK], vn[n, h])
             for n, h in pairs}
        for n, h in pairs:
            st_ref[n, h] = (st[n, h] * gl_ref[n, grow, sls[h]][0:1, :]
                            + _dot_tn(kd_ref[n, rows, sls[h]], vn[n, h]))
        for n, h in pairs:
            gate = z_ref[n, rows, sls[h]].astype(F32)
            o_ref[n, rows, sls[h]] = _head_norm_gate(o[n, h], ng_ref[...], gate).astype(o_ref.dtype)
        return carry

    lax.fori_loop(0, n_chunks, chunk, 0)


def _gdn(u_grp, u_small, conv_w, a_log_row, dt_row, e_beta, e_g, norm_g):
    bsz, seq, _ = u_grp.shape
    t = GDN_PREP_T
    assert seq % t == 0 and t % HALO == 0
    blk = lambda off: pl.BlockSpec((None, t, GDN_W), lambda b, i: (b, i, off // GDN_W))
    halo = lambda off: pl.BlockSpec(
        (None, HALO, GDN_W), lambda b, i: (b, jnp.maximum(i * (t // HALO) - 1, 0), off // GDN_W))
    full = lambda a: pl.BlockSpec(a.shape, lambda b, i: (0,) * a.ndim)
    out = lambda: pl.BlockSpec((None, t, GDN_W), lambda b, i: (b, i, 0))
    gl_rows = t // CHUNK * SUBLANE
    sds = lambda dt: jax.ShapeDtypeStruct((bsz, seq, GDN_W), dt)
    qg, kd, u, w, aqk, gl = pl.pallas_call(
        _gdn_prep_kernel,
        grid=(bsz, seq // t),
        in_specs=[blk(OFF_DQ), blk(OFF_DK), blk(OFF_DV), halo(OFF_DQ), halo(OFF_DK), halo(OFF_DV),
                  pl.BlockSpec((None, t, LANE), lambda b, i: (b, i, 0)),
                  full(conv_w), full(a_log_row), full(dt_row), full(e_beta), full(e_g)],
        out_specs=[out(), out(), out(), out(), out(),
                   pl.BlockSpec((None, gl_rows, GDN_W), lambda b, i: (b, i, 0))],
        out_shape=[sds(BF16), sds(BF16), sds(F32), sds(BF16), sds(BF16),
                   jax.ShapeDtypeStruct((bsz, seq // CHUNK * SUBLANE, GDN_W), F32)],
        scratch_shapes=[pltpu.VMEM((t, GDN_W), F32)] * 3,
        compiler_params=_params("parallel", "parallel"),
        name="gdn_prep",
    )(u_grp, u_grp, u_grp, u_grp, u_grp, u_grp, u_small, conv_w, a_log_row, dt_row, e_beta, e_g)

    ts = _pick(seq, (256, 128, 64))
    blk = lambda: pl.BlockSpec((bsz, ts, GDN_W), lambda i: (0, i, 0))
    return pl.pallas_call(
        functools.partial(_gdn_scan_kernel, n_chunks=ts // CHUNK),
        grid=(seq // ts,),
        in_specs=[blk(), blk(), blk(), blk(), blk(),
                  pl.BlockSpec((bsz, ts // CHUNK * SUBLANE, GDN_W), lambda i: (0, i, 0)),
                  pl.BlockSpec((bsz, ts, GDN_W), lambda i: (0, i, OFF_DZ // GDN_W)),
                  pl.BlockSpec(norm_g.shape, lambda i: (0, 0))],
        out_specs=blk(),
        out_shape=jax.ShapeDtypeStruct((bsz, seq, GDN_W), BF16),
        scratch_shapes=[pltpu.VMEM((bsz, GDN_HEADS, GDN_DK, GDN_DV), F32)],
        compiler_params=_params("arbitrary"),
        name="gdn_scan",
    )(qg, kd, u, w, aqk, gl, u_grp, norm_g)


SB_RB = 128
SB_LOOK = 2
SB_QT = 2 * SB_RB
SB_EXP_ZERO = 104.0


def _sb_kernel(q_ref, k_ref, v_ref, g_ref, ng_ref, o_ref, acc_ref, run_ref, z_buf, hl_buf, a_buf,
               *, n_invalid):
    seq = q_ref.shape[0]
    rb = SB_RB
    scale = SB_D ** -0.5
    jr = lax.broadcasted_iota(jnp.int32, (2 * rb, 2 * rb), 0)
    sc = lax.broadcasted_iota(jnp.int32, (2 * rb, 2 * rb), 1)
    rr = jnp.where((sc >= rb) | ((jr & (rb - 1)) >= sc), 1.0, 0.0).astype(BF16)
    row_i = lax.broadcasted_iota(jnp.int32, (rb, rb), 0)
    col_i = lax.broadcasted_iota(jnp.int32, (rb, rb), 1)
    below_diag = col_i < row_i

    def softplus(z):
        return jnp.maximum(z, 0.0) + jnp.log(1.0 + jnp.exp2(jnp.abs(z) * -LOG2E))

    def weights(z, sp, run):
        hi = sp.astype(BF16)
        lo = (sp - hi.astype(F32)).astype(BF16)
        cs = _dot(jnp.concatenate([hi, lo], axis=1), rr)
        return jnp.exp(z - cs[:, :rb] - run).astype(BF16), cs[:, rb:]

    def mask(z, sp, vis):
        return jnp.where(vis, z, -jnp.inf), jnp.where(vis, sp, 0.0)

    def finish(rows, acc):
        gate = g_ref[rows, :].astype(F32)
        o_ref[rows, :] = _head_norm_gate(acc, ng_ref[...], gate).astype(o_ref.dtype)

    def row_block(zs, key0, check_valid):
        run = jnp.zeros((rb, rb), F32)
        parts = [None] * len(zs)
        for c in reversed(range(len(zs))):
            z, sp = zs[c], softplus(zs[c])
            vis = below_diag if c == len(zs) - 1 else None
            if check_valid:
                ok = (key0 + c * rb + col_i) >= n_invalid
                vis = ok if vis is None else (vis & ok)
            if vis is not None:
                z, sp = mask(z, sp, vis)
            parts[c], tot = weights(z, sp, run)
            run = run + tot
        return jnp.concatenate(parts, axis=1), run

    def qtile(r0, looks, check_valid):
        k0 = r0 - looks[0] * rb
        nk = looks[0] + 2
        zz = _dot_nt(q_ref[pl.ds(r0, SB_QT), :], k_ref[pl.ds(k0, nk * rb), :]) * scale
        run_min = None
        for r in range(2):
            c0 = r - looks[r] + looks[0]
            zs = [zz[r * rb:(r + 1) * rb, c * rb:(c + 1) * rb] for c in range(c0, c0 + looks[r] + 1)]
            a, run = row_block(zs, k0 + c0 * rb, check_valid)
            acc = _dot(a, v_ref[pl.ds(k0 + c0 * rb, (looks[r] + 1) * rb), :])
            rows = pl.ds(r0 + r * rb, rb)
            acc_ref[rows, :] = acc
            run_ref[rows, :] = run
            finish(rows, acc)
            run_min = run if run_min is None else jnp.minimum(run_min, run)
        return run_min

    first_open = n_invalid // rb + SB_LOOK + 1
    assert first_open % 2 == 0 and first_open * rb <= seq and (first_open - SB_LOOK) * rb >= n_invalid
    qtile(0, (0, 1), True)
    for qi in range(1, first_open // 2):
        qtile(qi * SB_QT, (SB_LOOK, SB_LOOK), True)

    n_sub = SB_LOOK + 1
    t0 = first_open // 2
    n_tiles = seq // SB_QT - t0

    def tile_row(t):
        r0 = (t0 + t) * SB_QT
        return r0 if isinstance(t, int) else pl.multiple_of(r0, SB_QT)

    def scores(t):
        r0 = tile_row(t)
        k0 = r0 - SB_LOOK * rb
        zz = _dot_nt(q_ref[pl.ds(r0, SB_QT), :], k_ref[pl.ds(k0, (SB_LOOK + 2) * rb), :]) * scale
        for r in range(2):
            for c in range(n_sub):
                z = zz[r * rb:(r + 1) * rb, (r + c) * rb:(r + c + 1) * rb]
                sp = softplus(z)
                if c == n_sub - 1:
                    z, sp = mask(z, sp, below_diag)
                hi = sp.astype(BF16)
                lo = (sp - hi.astype(F32)).astype(BF16)
                z_buf[r, c] = z
                hl_buf[r, c] = jnp.concatenate([hi, lo], axis=1)

    def weights_of(t):
        r0 = tile_row(t)
        run_min = None
        for r in range(2):
            run = jnp.zeros((rb, rb), F32)
            for c in reversed(range(n_sub)):
                cs = _dot(hl_buf[r, c], rr)
                a_buf[r, :, c * rb:(c + 1) * rb] = jnp.exp(z_buf[r, c] - cs[:, :rb] - run).astype(BF16)
                run = run + cs[:, rb:]
            run_ref[pl.ds(r0 + r * rb, rb), :] = run
            run_min = run if run_min is None else jnp.minimum(run_min, run)
        return run_min

    def values(t):
        r0 = tile_row(t)
        k0 = r0 - SB_LOOK * rb
        for r in range(2):
            acc = _dot(a_buf[r], v_ref[pl.ds(k0 + r * rb, n_sub * rb), :])
            rows = pl.ds(r0 + r * rb, rb)
            acc_ref[rows, :] = acc
            finish(rows, acc)

    def step(s, run_min, stages):
        if stages[2]:
            values(s - 2)
        if stages[1]:
            run_min = jnp.minimum(run_min, weights_of(s - 1))
        if stages[0]:
            scores(s)
        return run_min

    run_min = jnp.full((rb, rb), 2 * SB_EXP_ZERO, F32)
    for s in range(n_tiles + 2):
        if s == 2 and n_tiles > 2:
            run_min = lax.fori_loop(2, n_tiles, lambda t, m: step(t, m, (True, True, True)), run_min,
                                    unroll=2)
        if 2 <= s < n_tiles:
            continue
        run_min = step(s, run_min, [0 <= s - k < n_tiles for k in (0, 1, 2)])

    @pl.when(jnp.min(run_min) < SB_EXP_ZERO)
    def _():
        def second_pass(b, carry):
            rows = pl.ds(pl.multiple_of(b * rb, rb), rb)

            def unfinished(state):
                j, lowest = state
                return (j >= 0) & (lowest < SB_EXP_ZERO)

            def sub_block(state):
                j, _ = state
                keys = pl.ds(pl.multiple_of(j * rb, rb), rb)
                z = _dot_nt(q_ref[rows, :], k_ref[keys, :]) * scale
                z, sp = mask(z, softplus(z), (j * rb + col_i) >= n_invalid)
                run = run_ref[rows, :]
                a, tot = weights(z, sp, run)
                acc_ref[rows, :] += _dot(a, v_ref[keys, :])
                run_ref[rows, :] = run + tot
                return j - 1, jnp.min(run + tot)

            lax.while_loop(unfinished, sub_block, (b - SB_LOOK - 1, jnp.min(run_ref[rows, :])))
            finish(rows, acc_ref[rows, :])
            return carry

        lax.fori_loop(first_open, seq // rb, second_pass, 0)


def _sb(u_grp, norm_g, n_invalid):
    bsz, seq, _ = u_grp.shape
    assert seq % SB_QT == 0 and SB_D == SB_RB
    blk = lambda off: pl.BlockSpec((None, seq, SB_D), lambda b, h: (b, 0, off // SB_D + h))
    return pl.pallas_call(
        functools.partial(_sb_kernel, n_invalid=n_invalid),
        grid=(bsz, SB_HEADS),
        in_specs=[blk(OFF_SQ), blk(OFF_SK), blk(OFF_SV), blk(OFF_SG),
                  pl.BlockSpec(norm_g.shape, lambda b, h: (0, 0))],
        out_specs=pl.BlockSpec((None, seq, SB_D), lambda b, h: (b, 0, h)),
        out_shape=jax.ShapeDtypeStruct((bsz, seq, SB_W), BF16),
        scratch_shapes=[pltpu.VMEM((seq, SB_D), F32), pltpu.VMEM((seq, SB_D), F32),
                        pltpu.VMEM((2, SB_LOOK + 1, SB_RB, SB_RB), F32),
                        pltpu.VMEM((2, SB_LOOK + 1, SB_RB, 2 * SB_RB), BF16),
                        pltpu.VMEM((2, SB_RB, (SB_LOOK + 1) * SB_RB), BF16)],
        compiler_params=_params("parallel", "parallel"),
        name="stick_breaking",
    )(u_grp, u_grp, u_grp, u_grp, norm_g)


def _place(vec, lane0):
    return jnp.zeros((1, LANE), F32).at[0, lane0:lane0 + vec.shape[0]].set(vec.astype(F32))


def _expand(lane0, heads, width):
    src = jnp.arange(LANE)[:, None]
    dst_head = jnp.arange(heads * width)[None, :] // width
    return (src == lane0 + dst_head).astype(BF16)


def _narrow_weights(w_t, layer):
    o = _IN_OFF
    base = layer * o[-1]
    return jnp.concatenate(
        [w_t[base + o[4]:base + o[5]], w_t[base + o[9]:base + o[11]],
         jnp.zeros((LANE - GLA_RANK - 2 * GDN_HEADS, w_t.shape[1]), w_t.dtype)], axis=0)


def _layer(h, xn, u_small, w_t, w_out, layer, gla_w_gate, gla_b_gate, gla_norm_g, gdn_conv_w,
           gdn_a_log, gdn_dt_bias, gdn_norm_g, sb_norm_g, bsz, seq):
    m = bsz * seq
    o = _IN_OFF
    base = layer * o[-1]

    def project(lo, hi, name):
        return _in_proj(xn, w_t, base + lo, hi - lo, name).reshape(bsz, seq, hi - lo)

    u_gla = project(o[0], o[4], "in_proj_gla")
    u_gdn = project(o[5], o[9], "in_proj_gdn")
    u_sb = project(o[11], o[15], "in_proj_sb")
    u_small = u_small.reshape(bsz, seq, LANE)

    w_gate_pad = jnp.zeros((LANE, GLA_QK), F32).at[S_LR:S_LR + GLA_RANK].set(gla_w_gate)
    o_gla = _gla(u_gla, u_small, w_gate_pad, gla_b_gate.reshape(1, GLA_QK),
                 gla_norm_g.reshape(1, GLA_DV))
    o_gdn = _gdn(u_gdn, u_small, gdn_conv_w, _place(gdn_a_log, S_A), _place(gdn_dt_bias, S_A),
                 _expand(S_B, GDN_HEADS, GDN_DK), _expand(S_A, GDN_HEADS, GDN_DK),
                 gdn_norm_g.reshape(1, GDN_DV))
    o_sb = _sb(u_sb, sb_norm_g.reshape(1, SB_D), PREFIX - N_META)

    return _out_proj(o_gla.reshape(m, GLA_W), o_gdn.reshape(m, GDN_W), o_sb.reshape(m, SB_W),
                     w_out, layer, h)


def kernel(x, meta, norm_g, w_in, gla_w_gate, gla_b_gate, gla_norm_g, gdn_conv_w, gdn_a_log,
           gdn_dt_bias, gdn_norm_g, sb_norm_g, w_out, final_g):
    bsz, n, d = x.shape
    prefix = jnp.concatenate([jnp.zeros((PREFIX - N_META, d), x.dtype), meta.astype(x.dtype)], axis=0)
    seq = PREFIX + n
    w_t = jnp.swapaxes(w_in, 1, 2).reshape(-1, d)
    h, xn, u_small = _embed_norm(x, prefix, norm_g[0], _narrow_weights(w_t, 0))
    h = h.reshape(bsz * seq, d)
    xn = xn.reshape(bsz * seq, d)
    for l in range(norm_g.shape[0]):
        if l > 0:
            xn, u_small = _rmsnorm_narrow(h, norm_g[l], _narrow_weights(w_t, l))
        h = _layer(h, xn, u_small, w_t, w_out, l, gla_w_gate[l], gla_b_gate[l], gla_norm_g[l], gdn_conv_w[l],
                   gdn_a_log[l], gdn_dt_bias[l], gdn_norm_g[l], sb_norm_g[l], bsz, seq)
    return _final_rmsnorm(h.reshape(bsz, seq, d), final_g, PREFIX)
```

```python
import functools

import jax
import jax.numpy as jnp
from jax import lax
from jax.experimental import pallas as pl
from jax.experimental.pallas import tpu as pltpu

F32 = jnp.float32
BF16 = jnp.bfloat16

N_META = 16
PREFIX = 256
CHUNK = 64
EPS = 1e-6
LOG2E = 1.4426950408889634

GLA_HEADS, GLA_DK, GLA_DV, GLA_RANK, GLA_TAU = 4, 128, 256, 16, 16.0
GDN_HEADS, GDN_DK, GDN_DV, CONV_K = 12, 128, 128, 4
SB_HEADS, SB_D = 12, 128

GLA_QK = GLA_HEADS * GLA_DK
GLA_W = GLA_HEADS * GLA_DV
GDN_QK = GDN_HEADS * GDN_DK
GDN_W = GDN_HEADS * GDN_DV
SB_W = SB_HEADS * SB_D

_IN_SPLITS = (GLA_QK, GLA_QK, GLA_W, GLA_W, GLA_RANK,
              GDN_QK, GDN_QK, GDN_W, GDN_W, GDN_HEADS, GDN_HEADS,
              SB_W, SB_W, SB_W, SB_W)
_IN_OFF = [0]
for _w in _IN_SPLITS:
    _IN_OFF.append(_IN_OFF[-1] + _w)

OFF_GQ, OFF_GK, OFF_GV, OFF_GR = 0, 512, 1024, 2048
OFF_DQ, OFF_DK, OFF_DV, OFF_DZ = 0, 1536, 3072, 4608
OFF_SQ, OFF_SK, OFF_SV, OFF_SG = 0, 1536, 3072, 4608
LANE = 128
SUBLANE = 8
S_LR, S_B, S_A = 0, GLA_RANK, GLA_RANK + GDN_HEADS

V7X_VMEM_LIMIT_BYTES = 56 * 1024 * 1024


def _pick(n, candidates):
    for c in candidates:
        if n % c == 0:
            return c
    raise ValueError(f"no block size in {candidates} divides {n}")


def _params(*sem):
    return pltpu.CompilerParams(dimension_semantics=sem, vmem_limit_bytes=V7X_VMEM_LIMIT_BYTES)


def _dot(a, b):
    return jnp.dot(a, b, preferred_element_type=F32)


def _dot_nt(a, b):
    return lax.dot_general(a, b, (((1,), (1,)), ((), ())), preferred_element_type=F32)


def _dot_tn(a, b):
    return lax.dot_general(a, b, (((0,), (0,)), ((), ())), preferred_element_type=F32)


def _split_bf16(x, n):
    parts = []
    for _ in range(n):
        p = x.astype(BF16)
        parts.append(p)
        x = x - p.astype(F32)
    return parts


def _dot_f32_by_exact(a, b_exact):
    hi, mid, lo = _split_bf16(a, 3)
    return _dot(hi, b_exact) + _dot(mid, b_exact) + _dot(lo, b_exact)


def _dot_exact_by_f32(a_exact, b):
    hi, mid, lo = _split_bf16(b, 3)
    return _dot(a_exact, hi) + _dot(a_exact, mid) + _dot(a_exact, lo)


def _dot_split2(a_parts, b_parts):
    return _dot(a_parts[0], b_parts[0]) + _dot(a_parts[0], b_parts[1]) + _dot(a_parts[1], b_parts[0])


def _exp_neg(x):
    return jnp.exp2(x * -LOG2E)


def _softplus(x):
    return jnp.maximum(x, 0.0) + jnp.log(1.0 + _exp_neg(jnp.abs(x)))


def _log_sigmoid(x):
    return jnp.minimum(x, 0.0) - jnp.log(1.0 + _exp_neg(jnp.abs(x)))


def _sigmoid(x):
    return 1.0 / (1.0 + _exp_neg(x))


def _head_norm_gate(o, norm_g, gate):
    ms = jnp.mean(o * o, axis=-1, keepdims=True)
    return o * lax.rsqrt(ms + EPS) * norm_g * (gate * _sigmoid(gate))


def _rmsnorm_kernel(x_ref, g_ref, o_ref):
    x = x_ref[...]
    ms = jnp.mean(x * x, axis=-1, keepdims=True)
    o_ref[...] = (x * lax.rsqrt(ms + EPS) * g_ref[...]).astype(o_ref.dtype)


def _norm_and_narrow(x, g_ref, ws_ref, o_ref, s_ref):
    ms = jnp.mean(x * x, axis=-1, keepdims=True)
    xn = (x * lax.rsqrt(ms + EPS) * g_ref[...]).astype(BF16)
    o_ref[...] = xn
    s_ref[...] = _dot_nt(xn, ws_ref[...].astype(BF16))


def _rmsnorm_narrow_kernel(x_ref, g_ref, ws_ref, o_ref, s_ref):
    _norm_and_narrow(x_ref[...], g_ref, ws_ref, o_ref, s_ref)


def _rmsnorm_narrow(x, g, w_small):
    m, d = x.shape
    tm = _pick(m, (512, 256, 128, 64, 8))
    return pl.pallas_call(
        _rmsnorm_narrow_kernel,
        grid=(m // tm,),
        in_specs=[pl.BlockSpec((tm, d), lambda i: (i, 0)),
                  pl.BlockSpec((1, d), lambda i: (0, 0)),
                  pl.BlockSpec(w_small.shape, lambda i: (0, 0))],
        out_specs=[pl.BlockSpec((tm, d), lambda i: (i, 0)),
                   pl.BlockSpec((tm, LANE), lambda i: (i, 0))],
        out_shape=[jax.ShapeDtypeStruct((m, d), BF16), jax.ShapeDtypeStruct((m, LANE), F32)],
        compiler_params=_params("parallel"),
        name="rmsnorm",
    )(x, g.reshape(1, d), w_small)


def _embed_norm_kernel(x_ref, p_ref, g_ref, ws_ref, h_ref, o_ref, s_ref):
    x = jnp.where(pl.program_id(1) == 0, p_ref[...], x_ref[...])
    h_ref[...] = x
    _norm_and_narrow(x, g_ref, ws_ref, o_ref, s_ref)


def _embed_norm(x, prefix, g, w_small):
    bsz, n, d = x.shape
    tm = prefix.shape[0]
    assert n % tm == 0
    blk = lambda w: pl.BlockSpec((None, tm, w), lambda b, i: (b, i, 0))
    return pl.pallas_call(
        _embed_norm_kernel,
        grid=(bsz, n // tm + 1),
        in_specs=[pl.BlockSpec((None, tm, d), lambda b, i: (b, jnp.maximum(i - 1, 0), 0)),
                  pl.BlockSpec((tm, d), lambda b, i: (0, 0)),
                  pl.BlockSpec((1, d), lambda b, i: (0, 0)),
                  pl.BlockSpec(w_small.shape, lambda b, i: (0, 0))],
        out_specs=[blk(d), blk(d), blk(LANE)],
        out_shape=[jax.ShapeDtypeStruct((bsz, n + tm, d), x.dtype),
                   jax.ShapeDtypeStruct((bsz, n + tm, d), BF16),
                   jax.ShapeDtypeStruct((bsz, n + tm, LANE), F32)],
        compiler_params=_params("parallel", "parallel"),
        name="embed_norm",
    )(x, prefix, g.reshape(1, d), w_small)


def _final_rmsnorm(h, g, n_skip):
    bsz, seq, d = h.shape
    tm = _pick(n_skip, (256, 128, 64, 8))
    assert (seq - n_skip) % tm == 0
    return pl.pallas_call(
        _rmsnorm_kernel,
        grid=(bsz, (seq - n_skip) // tm),
        in_specs=[pl.BlockSpec((None, tm, d), lambda b, i: (b, i + n_skip // tm, 0)),
                  pl.BlockSpec((1, d), lambda b, i: (0, 0))],
        out_specs=pl.BlockSpec((None, tm, d), lambda b, i: (b, i, 0)),
        out_shape=jax.ShapeDtypeStruct((bsz, seq - n_skip, d), h.dtype),
        compiler_params=_params("parallel", "parallel"),
        name="final_rmsnorm",
    )(h, g.reshape(1, d))


PROJ_TN = 512
CAST_ROWS = 512


def _cast_weight_tile(dst_ref, src_ref):
    step = min(CAST_ROWS, dst_ref.shape[0])

    def chunk(r, carry):
        rows = pl.ds(pl.multiple_of(r * step, step), step)
        dst_ref[rows, :] = src_ref[rows, :].astype(BF16)
        return carry

    lax.fori_loop(0, dst_ref.shape[0] // step, chunk, 0)


def _in_proj_kernel(x_ref, wt_ref, o_ref, w_ref):
    @pl.when(pl.program_id(1) == 0)
    def _():
        for c in range(wt_ref.shape[0] // LANE):
            cols = slice(c * LANE, (c + 1) * LANE)
            w_ref[:, cols] = wt_ref[cols, :].T.astype(BF16)

    o_ref[...] = _dot(x_ref[...], w_ref[...]).astype(o_ref.dtype)


def _in_proj(xn, w_t, row0, width, name):
    m, k = xn.shape
    tn = min(PROJ_TN, width)
    assert width % tn == 0 and row0 % SUBLANE == 0 and tn % LANE == 0
    tm = _pick(m, (1536, 1056, 768, 512, 256, 128))
    return pl.pallas_call(
        _in_proj_kernel,
        grid=(width // tn, m // tm),
        in_specs=[pl.BlockSpec((tm, k), lambda j, i: (i, 0)),
                  pl.BlockSpec((pl.Element(tn), pl.Element(k)),
                               lambda j, i: (pl.multiple_of(row0 + j * tn, SUBLANE), 0))],
        out_specs=pl.BlockSpec((tm, tn), lambda j, i: (i, j)),
        out_shape=jax.ShapeDtypeStruct((m, width), BF16),
        scratch_shapes=[pltpu.VMEM((k, tn), BF16)],
        compiler_params=_params("arbitrary", "arbitrary"),
        name=name,
    )(xn, w_t)


def _out_proj_kernel(a1_ref, a2_ref, a3_ref, wf_ref, h_ref, o_ref, w_ref):
    @pl.when(pl.program_id(1) == 0)
    def _():
        _cast_weight_tile(w_ref, wf_ref)

    k1, k2 = a1_ref.shape[1], a2_ref.shape[1]
    y = _dot(a1_ref[...], w_ref[0:k1, :])
    y = y + _dot(a2_ref[...], w_ref[k1:k1 + k2, :])
    y = y + _dot(a3_ref[...], w_ref[k1 + k2:, :])
    o_ref[...] = h_ref[...] + y


def _out_proj(a1, a2, a3, w_out, layer, h):
    m, d = h.shape
    k = w_out.shape[1]
    tn = PROJ_TN
    assert a1.shape[1] + a2.shape[1] + a3.shape[1] == k and d % tn == 0 and k % CAST_ROWS == 0
    tm = _pick(m, (1056, 768, 512, 256, 128))
    act = lambda a: pl.BlockSpec((tm, a.shape[1]), lambda j, i: (i, 0))
    return pl.pallas_call(
        _out_proj_kernel,
        grid=(d // tn, m // tm),
        in_specs=[act(a1), act(a2), act(a3),
                  pl.BlockSpec((None, k, tn), lambda j, i: (layer, 0, j)),
                  pl.BlockSpec((tm, tn), lambda j, i: (i, j))],
        out_specs=pl.BlockSpec((tm, tn), lambda j, i: (i, j)),
        out_shape=jax.ShapeDtypeStruct((m, d), F32),
        scratch_shapes=[pltpu.VMEM((k, tn), BF16)],
        compiler_params=_params("arbitrary", "arbitrary"),
        name="out_proj",
    )(a1, a2, a3, w_out, h)


def _gla_kernel(q_ref, k_ref, v_ref, r_ref, s_ref, wg_ref, bg_ref, ng_ref, o_ref, st_ref,
                *, n_chunks):
    @pl.when(pl.program_id(0) == 0)
    def _():
        st_ref[...] = jnp.zeros_like(st_ref)

    row = lax.broadcasted_iota(jnp.int32, (CHUNK, CHUNK), 0)
    col = lax.broadcasted_iota(jnp.int32, (CHUNK, CHUNK), 1)
    causal = row >= col
    tri = jnp.where(causal, 1.0, 0.0).astype(BF16)
    scale = GLA_DK ** -0.5
    wg_parts = _split_bf16(wg_ref[...], 2)
    ks = [slice(h * GLA_DK, (h + 1) * GLA_DK) for h in range(GLA_HEADS)]
    vs = [slice(h * GLA_DV, (h + 1) * GLA_DV) for h in range(GLA_HEADS)]
    pairs = [(n, h) for n in range(q_ref.shape[0]) for h in range(GLA_HEADS)]

    def chunk(c, carry):
        rows = pl.ds(pl.multiple_of(c * CHUNK, CHUNK), CHUNK)
        qt, kt, kl, dec = {}, {}, {}, {}
        for n in range(q_ref.shape[0]):
            x = _dot_split2(_split_bf16(s_ref[n, rows, :], 2), wg_parts) + bg_ref[...]
            g = _log_sigmoid(x) * (1.0 / GLA_TAU)
            b = _dot_exact_by_f32(tri, g)
            b_last = b[CHUNK - 1:CHUNK, :]
            q = q_ref[n, rows, :].astype(F32)
            k = k_ref[n, rows, :].astype(F32)
            qt[n] = (q * scale * jnp.exp(b)).astype(BF16)
            kt[n] = (k * jnp.exp(-b)).astype(BF16)
            kl[n] = (k * jnp.exp(b_last - b)).astype(BF16)
            dec[n] = jnp.exp(b_last)
        att = {(n, h): jnp.where(causal, _dot_nt(qt[n][:, ks[h]], kt[n][:, ks[h]]), 0.0).astype(BF16)
               for n, h in pairs}
        st = {(n, h): st_ref[n, h] for n, h in pairs}
        vh = {(n, h): v_ref[n, rows, vs[h]] for n, h in pairs}
        o = {(n, h): _dot(att[n, h], vh[n, h]) + _dot_nt(qt[n][:, ks[h]], st[n, h].astype(BF16))
             for n, h in pairs}
        for n, h in pairs:
            st_ref[n, h] = st[n, h] * dec[n][:, ks[h]] + _dot_tn(vh[n, h], kl[n][:, ks[h]])
        for n, h in pairs:
            gate = r_ref[n, rows, vs[h]].astype(F32)
            o_ref[n, rows, vs[h]] = _head_norm_gate(o[n, h], ng_ref[...], gate).astype(o_ref.dtype)
        return carry

    lax.fori_loop(0, n_chunks, chunk, 0, unroll=3)


def _gla(u_gla, u_small, w_gate_pad, b_gate, norm_g):
    bsz, seq, _ = u_gla.shape
    t = _pick(seq, (768, 384, 256, 128, 64))
    col = lambda off, w: pl.BlockSpec((bsz, t, w), lambda i: (0, i, off // w))
    full = lambda a: pl.BlockSpec(a.shape, lambda i: (0,) * a.ndim)
    return pl.pallas_call(
        functools.partial(_gla_kernel, n_chunks=t // CHUNK),
        grid=(seq // t,),
        in_specs=[col(OFF_GQ, GLA_QK), col(OFF_GK, GLA_QK), col(OFF_GV, GLA_W), col(OFF_GR, GLA_W),
                  pl.BlockSpec((bsz, t, LANE), lambda i: (0, i, 0)),
                  full(w_gate_pad), full(b_gate), full(norm_g)],
        out_specs=pl.BlockSpec((bsz, t, GLA_W), lambda i: (0, i, 0)),
        out_shape=jax.ShapeDtypeStruct((bsz, seq, GLA_W), BF16),
        scratch_shapes=[pltpu.VMEM((bsz, GLA_HEADS, GLA_DV, GLA_DK), F32)],
        compiler_params=_params("arbitrary"),
        name="gla",
    )(u_gla, u_gla, u_gla, u_gla, u_small, w_gate_pad, b_gate, norm_g)


GDN_PREP_T = 128
HALO = 16


def _gdn_prep_kernel(q_ref, k_ref, v_ref, qh_ref, kh_ref, vh_ref, s_ref, cw_ref, al_ref, dt_ref,
                     eb_ref, eg_ref, qg_ref, kd_ref, u_ref, w_ref, aqk_ref, gl_ref,
                     qs_ref, ks_ref, vs_ref):
    first = pl.program_id(1) == 0
    t = q_ref.shape[0]

    out_r = lax.broadcasted_iota(jnp.int32, ((CONV_K - 1) * t, HALO + t), 0)
    in_r = lax.broadcasted_iota(jnp.int32, ((CONV_K - 1) * t, HALO + t), 1)
    t_log2 = t.bit_length() - 1
    assert 1 << t_log2 == t
    shift_op = jnp.where(in_r == HALO + (out_r & (t - 1)) - ((out_r >> t_log2) + 1),
                         1.0, 0.0).astype(BF16)

    def conv_silu(x_ref, h_ref, w):
        x = x_ref[...]
        hist = jnp.where(first, jnp.zeros_like(h_ref[...]), h_ref[...])
        taps = _dot(shift_op, jnp.concatenate([hist, x], axis=0))
        y = x.astype(F32) * w[CONV_K - 1:CONV_K, :]
        for j in range(1, CONV_K):
            y = y + taps[(j - 1) * t:j * t, :] * w[CONV_K - 1 - j:CONV_K - j, :]
        return y * _sigmoid(y)

    qc = conv_silu(q_ref, qh_ref, cw_ref[:, 0:GDN_QK])
    kc = conv_silu(k_ref, kh_ref, cw_ref[:, GDN_QK:2 * GDN_QK])
    vs_ref[...] = conv_silu(v_ref, vh_ref, cw_ref[:, 2 * GDN_QK:2 * GDN_QK + GDN_W])
    for h in range(GDN_HEADS):
        sl = slice(h * GDN_DK, (h + 1) * GDN_DK)
        qh = qc[:, sl]
        kh = kc[:, sl]
        qs_ref[:, sl] = qh * (lax.rsqrt(jnp.sum(qh * qh, axis=-1, keepdims=True) + EPS)
                              * GDN_DK ** -0.5)
        ks_ref[:, sl] = kh * lax.rsqrt(jnp.sum(kh * kh, axis=-1, keepdims=True) + EPS)

    s = s_ref[...]
    beta = _sigmoid(s)
    g = -jnp.exp(al_ref[...]) * _softplus(s + dt_ref[...])

    row = lax.broadcasted_iota(jnp.int32, (CHUNK, CHUNK), 0)
    col = lax.broadcasted_iota(jnp.int32, (CHUNK, CHUNK), 1)
    causal = row >= col
    strict = row > col
    tri = jnp.where(causal, 1.0, 0.0).astype(BF16)
    zeros_half = jnp.zeros((CHUNK, GDN_DV - CHUNK), F32)

    pairs = [(c, h) for c in range(t // CHUNK) for h in range(GDN_HEADS)]
    a_neg, rhs = {}, {}
    for c in range(t // CHUNK):
        rows = slice(c * CHUNK, (c + 1) * CHUNK)
        gcum = _dot_exact_by_f32(tri, g[rows])
        gcum_t = gcum.T
        gb = _dot_f32_by_exact(gcum, eg_ref[...])
        bb = _dot_f32_by_exact(beta[rows], eb_ref[...])
        g_last = gb[CHUNK - 1:CHUNK, :]
        eg = jnp.exp(gb)
        q = qs_ref[rows, :]
        k = ks_ref[rows, :]
        kb = k * bb
        qg_ref[rows, :] = (q * eg).astype(BF16)
        kd_ref[rows, :] = (k * jnp.exp(g_last - gb)).astype(BF16)
        gl_ref[c * SUBLANE:(c + 1) * SUBLANE, :] = jnp.broadcast_to(jnp.exp(g_last), (SUBLANE, GDN_W))
        rhs_v = vs_ref[rows, :] * bb
        rhs_k = kb * eg
        for h in range(GDN_HEADS):
            sl = slice(h * GDN_DK, (h + 1) * GDN_DK)
            lhs = jnp.concatenate([kb[:, sl], q[:, sl]], axis=0).astype(BF16)
            pr = _dot_nt(lhs, k[:, sl].astype(BF16))
            g_t = gb[:, h * GDN_DK:h * GDN_DK + CHUNK]
            g_s = gcum_t[S_A + h:S_A + h + 1, :]
            dec = jnp.exp(jnp.where(causal, g_t - g_s, -jnp.inf))
            a_neg[c, h] = jnp.where(strict, -pr[:CHUNK] * dec, 0.0)
            aqk_ref[rows, sl] = jnp.concatenate([pr[CHUNK:] * dec, zeros_half], axis=1).astype(BF16)
            rhs[c, h] = jnp.concatenate([rhs_v[:, sl], rhs_k[:, sl]], axis=1)
    toff = dict(a_neg)
    p = dict(a_neg)
    for _ in range(5):
        for key in pairs:
            pb = p[key].astype(BF16)
            p[key] = _dot(pb, pb)
        for key in pairs:
            toff[key] = toff[key] + p[key] + _dot(toff[key].astype(BF16), p[key].astype(BF16))
    for c, h in pairs:
        rows = slice(c * CHUNK, (c + 1) * CHUNK)
        sl = slice(h * GDN_DK, (h + 1) * GDN_DK)
        sol = rhs[c, h] + _dot(toff[c, h].astype(BF16), rhs[c, h].astype(BF16))
        u_ref[rows, sl] = sol[:, :GDN_DV]
        w_ref[rows, sl] = sol[:, GDN_DV:].astype(BF16)


def _gdn_scan_kernel(qg_ref, kd_ref, u_ref, w_ref, aqk_ref, gl_ref, z_ref, ng_ref, o_ref, st_ref,
                     *, n_chunks):
    @pl.when(pl.program_id(0) == 0)
    def _():
        st_ref[...] = jnp.zeros_like(st_ref)

    sls = [slice(h * GDN_DK, (h + 1) * GDN_DK) for h in range(GDN_HEADS)]
    pairs = [(n, h) for n in range(qg_ref.shape[0]) for h in range(GDN_HEADS)]

    def chunk(c, carry):
        rows = pl.ds(pl.multiple_of(c * CHUNK, CHUNK), CHUNK)
        grow = pl.ds(pl.multiple_of(c * SUBLANE, SUBLANE), SUBLANE)
        st = {(n, h): st_ref[n, h] for n, h in pairs}
        ws = {(n, h): _dot(jnp.concatenate([w_ref[n, rows, sls[h]], qg_ref[n, rows, sls[h]]], axis=0),
                           st[n, h].astype(BF16)) for n, h in pairs}
        vn = {(n, h): (u_ref[n, rows, sls[h]] - ws[n, h][:CHUNK]).astype(BF16) for n, h in pairs}
        o = {(n, h): ws[n, h][CHUNK:] + _dot(aqk_ref[n, rows, h * GDN_DK:h * GDN_DK + CHUNK], vn[n, h])
             for n, h in pairs}
        for n, h in pairs:
            st_ref[n, h] = (st[n, h] * gl_ref[n, grow, sls[h]][0:1, :]
                            + _dot_tn(kd_ref[n, rows, sls[h]], vn[n, h]))
        for n, h in pairs:
            gate = z_ref[n, rows, sls[h]].astype(F32)
            o_ref[n, rows, sls[h]] = _head_norm_gate(o[n, h], ng_ref[...], gate).astype(o_ref.dtype)
        return carry

    lax.fori_loop(0, n_chunks, chunk, 0)


def _gdn(u_grp, u_small, conv_w, a_log_row, dt_row, e_beta, e_g, norm_g):
    bsz, seq, _ = u_grp.shape
    t = GDN_PREP_T
    assert seq % t == 0 and t % HALO == 0
    blk = lambda off: pl.BlockSpec((None, t, GDN_W), lambda b, i: (b, i, off // GDN_W))
    halo = lambda off: pl.BlockSpec(
        (None, HALO, GDN_W), lambda b, i: (b, jnp.maximum(i * (t // HALO) - 1, 0), off // GDN_W))
    full = lambda a: pl.BlockSpec(a.shape, lambda b, i: (0,) * a.ndim)
    out = lambda: pl.BlockSpec((None, t, GDN_W), lambda b, i: (b, i, 0))
    gl_rows = t // CHUNK * SUBLANE
    sds = lambda dt: jax.ShapeDtypeStruct((bsz, seq, GDN_W), dt)
    qg, kd, u, w, aqk, gl = pl.pallas_call(
        _gdn_prep_kernel,
        grid=(bsz, seq // t),
        in_specs=[blk(OFF_DQ), blk(OFF_DK), blk(OFF_DV), halo(OFF_DQ), halo(OFF_DK), halo(OFF_DV),
                  pl.BlockSpec((None, t, LANE), lambda b, i: (b, i, 0)),
                  full(conv_w), full(a_log_row), full(dt_row), full(e_beta), full(e_g)],
        out_specs=[out(), out(), out(), out(), out(),
                   pl.BlockSpec((None, gl_rows, GDN_W), lambda b, i: (b, i, 0))],
        out_shape=[sds(BF16), sds(BF16), sds(F32), sds(BF16), sds(BF16),
                   jax.ShapeDtypeStruct((bsz, seq // CHUNK * SUBLANE, GDN_W), F32)],
        scratch_shapes=[pltpu.VMEM((t, GDN_W), F32)] * 3,
        compiler_params=_params("parallel", "parallel"),
        name="gdn_prep",
    )(u_grp, u_grp, u_grp, u_grp, u_grp, u_grp, u_small, conv_w, a_log_row, dt_row, e_beta, e_g)

    ts = _pick(seq, (256, 128, 64))
    blk = lambda: pl.BlockSpec((bsz, ts, GDN_W), lambda i: (0, i, 0))
    return pl.pallas_call(
        functools.partial(_gdn_scan_kernel, n_chunks=ts // CHUNK),
        grid=(seq // ts,),
        in_specs=[blk(), blk(), blk(), blk(), blk(),
                  pl.BlockSpec((bsz, ts // CHUNK * SUBLANE, GDN_W), lambda i: (0, i, 0)),
                  pl.BlockSpec((bsz, ts, GDN_W), lambda i: (0, i, OFF_DZ // GDN_W)),
                  pl.BlockSpec(norm_g.shape, lambda i: (0, 0))],
        out_specs=blk(),
        out_shape=jax.ShapeDtypeStruct((bsz, seq, GDN_W), BF16),
        scratch_shapes=[pltpu.VMEM((bsz, GDN_HEADS, GDN_DK, GDN_DV), F32)],
        compiler_params=_params("arbitrary"),
        name="gdn_scan",
    )(qg, kd, u, w, aqk, gl, u_grp, norm_g)


SB_RB = 128
SB_LOOK = 2
SB_QT = 2 * SB_RB
SB_EXP_ZERO = 104.0


def _sb_kernel(q_ref, k_ref, v_ref, g_ref, ng_ref, o_ref, acc_ref, run_ref, z_buf, hl_buf, a_buf,
               *, n_invalid):
    seq = q_ref.shape[0]
    rb = SB_RB
    scale = SB_D ** -0.5
    jr = lax.broadcasted_iota(jnp.int32, (2 * rb, 2 * rb), 0)
    sc = lax.broadcasted_iota(jnp.int32, (2 * rb, 2 * rb), 1)
    rr = jnp.where((sc >= rb) | ((jr & (rb - 1)) >= sc), 1.0, 0.0).astype(BF16)
    row_i = lax.broadcasted_iota(jnp.int32, (rb, rb), 0)
    col_i = lax.broadcasted_iota(jnp.int32, (rb, rb), 1)
    below_diag = col_i < row_i

    def softplus(z):
        return jnp.maximum(z, 0.0) + jnp.log(1.0 + jnp.exp2(jnp.abs(z) * -LOG2E))

    def weights(z, sp, run):
        hi = sp.astype(BF16)
        lo = (sp - hi.astype(F32)).astype(BF16)
        cs = _dot(jnp.concatenate([hi, lo], axis=1), rr)
        return jnp.exp(z - cs[:, :rb] - run).astype(BF16), cs[:, rb:]

    def mask(z, sp, vis):
        return jnp.where(vis, z, -jnp.inf), jnp.where(vis, sp, 0.0)

    def finish(rows, acc):
        gate = g_ref[rows, :].astype(F32)
        o_ref[rows, :] = _head_norm_gate(acc, ng_ref[...], gate).astype(o_ref.dtype)

    def row_block(zs, key0, check_valid):
        run = jnp.zeros((rb, rb), F32)
        parts = [None] * len(zs)
        for c in reversed(range(len(zs))):
            z, sp = zs[c], softplus(zs[c])
            vis = below_diag if c == len(zs) - 1 else None
            if check_valid:
                ok = (key0 + c * rb + col_i) >= n_invalid
                vis = ok if vis is None else (vis & ok)
            if vis is not None:
                z, sp = mask(z, sp, vis)
            parts[c], tot = weights(z, sp, run)
            run = run + tot
        return jnp.concatenate(parts, axis=1), run

    def qtile(r0, looks, check_valid):
        k0 = r0 - looks[0] * rb
        nk = looks[0] + 2
        zz = _dot_nt(q_ref[pl.ds(r0, SB_QT), :], k_ref[pl.ds(k0, nk * rb), :]) * scale
        run_min = None
        for r in range(2):
            c0 = r - looks[r] + looks[0]
            zs = [zz[r * rb:(r + 1) * rb, c * rb:(c + 1) * rb] for c in range(c0, c0 + looks[r] + 1)]
            a, run = row_block(zs, k0 + c0 * rb, check_valid)
            acc = _dot(a, v_ref[pl.ds(k0 + c0 * rb, (looks[r] + 1) * rb), :])
            rows = pl.ds(r0 + r * rb, rb)
            acc_ref[rows, :] = acc
            run_ref[rows, :] = run
            finish(rows, acc)
            run_min = run if run_min is None else jnp.minimum(run_min, run)
        return run_min

    first_open = n_invalid // rb + SB_LOOK + 1
    assert first_open % 2 == 0 and first_open * rb <= seq and (first_open - SB_LOOK) * rb >= n_invalid
    qtile(0, (0, 1), True)
    for qi in range(1, first_open // 2):
        qtile(qi * SB_QT, (SB_LOOK, SB_LOOK), True)

    n_sub = SB_LOOK + 1
    t0 = first_open // 2
    n_tiles = seq // SB_QT - t0

    def tile_row(t):
        r0 = (t0 + t) * SB_QT
        return r0 if isinstance(t, int) else pl.multiple_of(r0, SB_QT)

    def scores(t):
        r0 = tile_row(t)
        k0 = r0 - SB_LOOK * rb
        zz = _dot_nt(q_ref[pl.ds(r0, SB_QT), :], k_ref[pl.ds(k0, (SB_LOOK + 2) * rb), :]) * scale
        for r in range(2):
            for c in range(n_sub):
                z = zz[r * rb:(r + 1) * rb, (r + c) * rb:(r + c + 1) * rb]
                sp = softplus(z)
                if c == n_sub - 1:
                    z, sp = mask(z, sp, below_diag)
                hi = sp.astype(BF16)
                lo = (sp - hi.astype(F32)).astype(BF16)
                z_buf[r, c] = z
                hl_buf[r, c] = jnp.concatenate([hi, lo], axis=1)

    def weights_of(t):
        r0 = tile_row(t)
        run_min = None
        for r in range(2):
            run = jnp.zeros((rb, rb), F32)
            for c in reversed(range(n_sub)):
                cs = _dot(hl_buf[r, c], rr)
                a_buf[r, :, c * rb:(c + 1) * rb] = jnp.exp(z_buf[r, c] - cs[:, :rb] - run).astype(BF16)
                run = run + cs[:, rb:]
            run_ref[pl.ds(r0 + r * rb, rb), :] = run
            run_min = run if run_min is None else jnp.minimum(run_min, run)
        return run_min

    def values(t):
        r0 = tile_row(t)
        k0 = r0 - SB_LOOK * rb
        for r in range(2):
            acc = _dot(a_buf[r], v_ref[pl.ds(k0 + r * rb, n_sub * rb), :])
            rows = pl.ds(r0 + r * rb, rb)
            acc_ref[rows, :] = acc
            finish(rows, acc)

    def step(s, run_min, stages):
        if stages[2]:
            values(s - 2)
        if stages[1]:
            run_min = jnp.minimum(run_min, weights_of(s - 1))
        if stages[0]:
            scores(s)
        return run_min

    run_min = jnp.full((rb, rb), 2 * SB_EXP_ZERO, F32)
    for s in range(n_tiles + 2):
        if s == 2 and n_tiles > 2:
            run_min = lax.fori_loop(2, n_tiles, lambda t, m: step(t, m, (True, True, True)), run_min,
                                    unroll=4)
        if 2 <= s < n_tiles:
            continue
        run_min = step(s, run_min, [0 <= s - k < n_tiles for k in (0, 1, 2)])

    @pl.when(jnp.min(run_min) < SB_EXP_ZERO)
    def _():
        def second_pass(b, carry):
            rows = pl.ds(pl.multiple_of(b * rb, rb), rb)

            def unfinished(state):
                j, lowest = state
                return (j >= 0) & (lowest < SB_EXP_ZERO)

            def sub_block(state):
                j, _ = state
                keys = pl.ds(pl.multiple_of(j * rb, rb), rb)
                z = _dot_nt(q_ref[rows, :], k_ref[keys, :]) * scale
                z, sp = mask(z, softplus(z), (j * rb + col_i) >= n_invalid)
                run = run_ref[rows, :]
                a, tot = weights(z, sp, run)
                acc_ref[rows, :] += _dot(a, v_ref[keys, :])
                run_ref[rows, :] = run + tot
                return j - 1, jnp.min(run + tot)

            lax.while_loop(unfinished, sub_block, (b - SB_LOOK - 1, jnp.min(run_ref[rows, :])))
            finish(rows, acc_ref[rows, :])
            return carry

        lax.fori_loop(first_open, seq // rb, second_pass, 0)


def _sb(u_grp, norm_g, n_invalid):
    bsz, seq, _ = u_grp.shape
    assert seq % SB_QT == 0 and SB_D == SB_RB
    blk = lambda off: pl.BlockSpec((None, seq, SB_D), lambda b, h: (b, 0, off // SB_D + h))
    return pl.pallas_call(
        functools.partial(_sb_kernel, n_invalid=n_invalid),
        grid=(bsz, SB_HEADS),
        in_specs=[blk(OFF_SQ), blk(OFF_SK), blk(OFF_SV), blk(OFF_SG),
                  pl.BlockSpec(norm_g.shape, lambda b, h: (0, 0))],
        out_specs=pl.BlockSpec((None, seq, SB_D), lambda b, h: (b, 0, h)),
        out_shape=jax.ShapeDtypeStruct((bsz, seq, SB_W), BF16),
        scratch_shapes=[pltpu.VMEM((seq, SB_D), F32), pltpu.VMEM((seq, SB_D), F32),
                        pltpu.VMEM((2, SB_LOOK + 1, SB_RB, SB_RB), F32),
                        pltpu.VMEM((2, SB_LOOK + 1, SB_RB, 2 * SB_RB), BF16),
                        pltpu.VMEM((2, SB_RB, (SB_LOOK + 1) * SB_RB), BF16)],
        compiler_params=_params("parallel", "parallel"),
        name="stick_breaking",
    )(u_grp, u_grp, u_grp, u_grp, norm_g)


def _place(vec, lane0):
    return jnp.zeros((1, LANE), F32).at[0, lane0:lane0 + vec.shape[0]].set(vec.astype(F32))


def _expand(lane0, heads, width):
    src = jnp.arange(LANE)[:, None]
    dst_head = jnp.arange(heads * width)[None, :] // width
    return (src == lane0 + dst_head).astype(BF16)


def _narrow_weights(w_t, layer):
    o = _IN_OFF
    base = layer * o[-1]
    return jnp.concatenate(
        [w_t[base + o[4]:base + o[5]], w_t[base + o[9]:base + o[11]],
         jnp.zeros((LANE - GLA_RANK - 2 * GDN_HEADS, w_t.shape[1]), w_t.dtype)], axis=0)


def _layer(h, xn, u_small, w_t, w_out, layer, gla_w_gate, gla_b_gate, gla_norm_g, gdn_conv_w,
           gdn_a_log, gdn_dt_bias, gdn_norm_g, sb_norm_g, bsz, seq):
    m = bsz * seq
    o = _IN_OFF
    base = layer * o[-1]

    def project(lo, hi, name):
        return _in_proj(xn, w_t, base + lo, hi - lo, name).reshape(bsz, seq, hi - lo)

    u_gla = project(o[0], o[4], "in_proj_gla")
    u_gdn = project(o[5], o[9], "in_proj_gdn")
    u_sb = project(o[11], o[15], "in_proj_sb")
    u_small = u_small.reshape(bsz, seq, LANE)

    w_gate_pad = jnp.zeros((LANE, GLA_QK), F32).at[S_LR:S_LR + GLA_RANK].set(gla_w_gate)
    o_gla = _gla(u_gla, u_small, w_gate_pad, gla_b_gate.reshape(1, GLA_QK),
                 gla_norm_g.reshape(1, GLA_DV))
    o_gdn = _gdn(u_gdn, u_small, gdn_conv_w, _place(gdn_a_log, S_A), _place(gdn_dt_bias, S_A),
                 _expand(S_B, GDN_HEADS, GDN_DK), _expand(S_A, GDN_HEADS, GDN_DK),
                 gdn_norm_g.reshape(1, GDN_DV))
    o_sb = _sb(u_sb, sb_norm_g.reshape(1, SB_D), PREFIX - N_META)

    return _out_proj(o_gla.reshape(m, GLA_W), o_gdn.reshape(m, GDN_W), o_sb.reshape(m, SB_W),
                     w_out, layer, h)


def kernel(x, meta, norm_g, w_in, gla_w_gate, gla_b_gate, gla_norm_g, gdn_conv_w, gdn_a_log,
           gdn_dt_bias, gdn_norm_g, sb_norm_g, w_out, final_g):
    bsz, n, d = x.shape
    prefix = jnp.concatenate([jnp.zeros((PREFIX - N_META, d), x.dtype), meta.astype(x.dtype)], axis=0)
    seq = PREFIX + n
    w_t = jnp.swapaxes(w_in, 1, 2).reshape(-1, d)
    h, xn, u_small = _embed_norm(x, prefix, norm_g[0], _narrow_weights(w_t, 0))
    h = h.reshape(bsz * seq, d)
    xn = xn.reshape(bsz * seq, d)
    for l in range(norm_g.shape[0]):
        if l > 0:
            xn, u_small = _rmsnorm_narrow(h, norm_g[l], _narrow_weights(w_t, l))
        h = _layer(h, xn, u_small, w_t, w_out, l, gla_w_gate[l], gla_b_gate[l], gla_norm_g[l], gdn_conv_w[l],
                   gdn_a_log[l], gdn_dt_bias[l], gdn_norm_g[l], sb_norm_g[l], bsz, seq)
    return _final_rmsnorm(h.reshape(bsz, seq, d), final_g, PREFIX)
```
